```python
import jax
import jax.numpy as jnp
from jax import lax

D_MODEL = 4096
BATCH = 8
SEQ = 2048
DEPTH = 2
DEC_BATCH = 16
DEC_SEQ = 32
PAST_LEN = 2048

CHUNK = 64
HEAD_DIM = 128
A_HEADS = D_MODEL // (2 * HEAD_DIM)
A_PREV_CHUNKS = 8
A_REACH = A_PREV_CHUNKS * CHUNK
REL_CLIP = 128
B_HEADS = D_MODEL // (2 * HEAD_DIM)
B_KV_HEADS = B_HEADS // 4
B_WINDOW = 128
B_PREV_CHUNKS = B_WINDOW // CHUNK
B_REACH = B_PREV_CHUNKS * CHUNK
C_HEADS = D_MODEL // HEAD_DIM
Q_BLOCK = 128
FORGET_BIAS = 3.0
FFN_DIM = 256 * ((8 * D_MODEL // 3 + 255) // 256)
N_EXPERTS = 8
TOP_K = 2
EXPERT_DIM = 7 * D_MODEL // 2
N_EVEN = (DEPTH + 1) // 2
N_ODD = DEPTH // 2
EPS = 1e-6
A_WIDTH = A_HEADS * HEAD_DIM
B_WIDTH = B_HEADS * HEAD_DIM
B_KV_WIDTH = B_KV_HEADS * HEAD_DIM
C_WIDTH = C_HEADS * HEAD_DIM
AB_IN = 3 * A_WIDTH + B_WIDTH + 2 * B_KV_WIDTH
C_IN = 3 * C_WIDTH + C_HEADS

kernel_name = "streaming_hybrid_encoder_step"

F32 = jnp.float32


def rms_norm(x, g):
    xf = x.astype(F32)
    y = xf * lax.rsqrt(jnp.mean(xf * xf, axis=-1, keepdims=True) + EPS) * g.astype(F32)
    return y.astype(x.dtype)


def adaln_params(c, w, b):
    mod = jax.nn.silu(c) @ w + b
    return jnp.split(mod[:, None, :], 6, axis=-1)


def modulate(x, g, shift, scale):
    return rms_norm(x, g) * (1.0 + scale) + shift


def split_heads(z, n):
    return z.reshape(z.shape[0], z.shape[1], n, HEAD_DIM)


def swiglu(h, w_gate, w_up, w_down):
    return (jax.nn.silu(h @ w_gate) * (h @ w_up)) @ w_down


def moe_swiglu(h, router_w, router_b, w_gate, w_up, w_down):
    logits = (h @ router_w).astype(F32) + router_b.astype(F32)
    top_v, top_i = lax.top_k(logits, TOP_K)
    gates = jax.nn.softmax(top_v, axis=-1)
    combine = jnp.sum(jax.nn.one_hot(top_i, N_EXPERTS, dtype=F32) * gates[..., None], axis=-2)
    y = jnp.zeros_like(h)
    for e in range(N_EXPERTS):
        y = y + combine[..., e:e + 1].astype(h.dtype) * swiglu(h, w_gate[e], w_up[e], w_down[e])
    return y


def chunk_band_mask(q_pos, k_pos, n_prev):
    qc = q_pos[:, None] // CHUNK
    kc = k_pos[None, :] // CHUNK
    return (kc <= qc) & (kc >= qc - n_prev) & (k_pos[None, :] >= 0)


def rel_bias(table, q_pos, k_pos):
    d = jnp.clip(q_pos[:, None] - k_pos[None, :], -REL_CLIP, REL_CLIP) + REL_CLIP
    return jnp.moveaxis(table[d], -1, 0).astype(F32)


def alibi_bias(n_heads, q_pos, k_pos):
    slopes = 2.0 ** (-8.0 * jnp.arange(1, n_heads + 1, dtype=F32) / n_heads)
    dist = jnp.abs(q_pos[:, None] - k_pos[None, :]).astype(F32)
    return -slopes[:, None, None] * dist


def band_core(q, k, v, bias, allowed, sinks):
    b, tq, h, dh = q.shape
    hkv = k.shape[2]
    g = h // hkv
    qg = q.reshape(b, tq, hkv, g, dh)
    s = jnp.einsum('bqngd,bknd->bngqk', qg, k, preferred_element_type=F32) * (dh ** -0.5)
    s = s + bias.reshape(hkv, g, tq, -1)
    s = jnp.where(allowed, s, -jnp.inf)
    if sinks is None:
        p = jax.nn.softmax(s, axis=-1)
    else:
        sink = sinks.astype(F32).reshape(hkv, g, 1, 1)
        m = jnp.maximum(jnp.max(s, axis=-1, keepdims=True), sink)
        e = jnp.exp(s - m)
        p = e / (jnp.sum(e, axis=-1, keepdims=True) + jnp.exp(sink - m))
    out = jnp.einsum('bngqk,bknd->bqngd', p.astype(v.dtype), v)
    return out.reshape(b, tq, h, dh)


def band_attention_prompt(q, k, v, n_prev, bias_fn, sinks):
    b, s, h, dh = q.shape
    reach = n_prev * CHUNK
    kp = jnp.pad(k, ((0, 0), (reach, 0), (0, 0), (0, 0)))
    vp = jnp.pad(v, ((0, 0), (reach, 0), (0, 0), (0, 0)))

    def one_chunk(c):
        start = c * CHUNK
        q_c = lax.dynamic_slice_in_dim(q, start, CHUNK, axis=1)
        k_c = lax.dynamic_slice_in_dim(kp, start, reach + CHUNK, axis=1)
        v_c = lax.dynamic_slice_in_dim(vp, start, reach + CHUNK, axis=1)
        q_pos = start + jnp.arange(CHUNK)
        k_pos = start - reach + jnp.arange(reach + CHUNK)
        return band_core(q_c, k_c, v_c, bias_fn(q_pos, k_pos), chunk_band_mask(q_pos, k_pos, n_prev), sinks)

    out = lax.map(one_chunk, jnp.arange(s // CHUNK))
    return jnp.moveaxis(out, 0, 1).reshape(b, s, h, dh)


def band_attention_sample(q, k_new, v_new, cache_k, cache_v, past_len, n_prev, bias_fn, sinks):
    t = q.shape[1]
    n_cache = cache_k.shape[1]
    k = jnp.concatenate([cache_k, k_new], axis=1)
    v = jnp.concatenate([cache_v, v_new], axis=1)
    q_pos = past_len + jnp.arange(t)
    k_pos = past_len - n_cache + jnp.arange(n_cache + t)
    out = band_core(q, k, v, bias_fn(q_pos, k_pos), chunk_band_mask(q_pos, k_pos, n_prev), sinks)
    return out, k[:, -n_cache:], v[:, -n_cache:]


def project_ab(h, w_in):
    z = h @ w_in
    qa, ka, va, qb, kb, vb = jnp.split(
        z, [A_WIDTH, 2 * A_WIDTH, 3 * A_WIDTH, 3 * A_WIDTH + B_WIDTH, 3 * A_WIDTH + B_WIDTH + B_KV_WIDTH], axis=-1)
    return (split_heads(qa, A_HEADS), split_heads(ka, A_HEADS), split_heads(va, A_HEADS),
            split_heads(qb, B_HEADS), split_heads(kb, B_KV_HEADS), split_heads(vb, B_KV_HEADS))


def merge_ab(oa, ob, w_out):
    b, t = oa.shape[:2]
    return jnp.concatenate([oa.reshape(b, t, A_WIDTH), ob.reshape(b, t, B_WIDTH)], axis=-1) @ w_out


def mixer_ab_prompt(h, w_in, rel_table, sinks, w_out):
    t = h.shape[1]
    qa, ka, va, qb, kb, vb = project_ab(h, w_in)
    oa = band_attention_prompt(qa, ka, va, A_PREV_CHUNKS, lambda qp, kp: rel_bias(rel_table, qp, kp), None)
    ob = band_attention_prompt(qb, kb, vb, B_PREV_CHUNKS, lambda qp, kp: alibi_bias(B_HEADS, qp, kp), sinks)
    la = min(A_REACH, t)
    lb = min(B_REACH, t)
    return merge_ab(oa, ob, w_out), ka[:, -la:], va[:, -la:], kb[:, -lb:], vb[:, -lb:]


def mixer_ab_sample(h, cache_a_k, cache_a_v, cache_b_k, cache_b_v, past_len, w_in, rel_table, sinks, w_out):
    qa, ka, va, qb, kb, vb = project_ab(h, w_in)
    oa, ak, av = band_attention_sample(qa, ka, va, cache_a_k, cache_a_v, past_len, A_PREV_CHUNKS,
                                       lambda qp, kp: rel_bias(rel_table, qp, kp), None)
    ob, bk, bv = band_attention_sample(qb, kb, vb, cache_b_k, cache_b_v, past_len, B_PREV_CHUNKS,
                                       lambda qp, kp: alibi_bias(B_HEADS, qp, kp), sinks)
    return merge_ab(oa, ob, w_out), ak, av, bk, bv


def project_c(h, w_in, b_forget):
    z = h @ w_in
    q, k, v, g = jnp.split(z, [C_WIDTH, 2 * C_WIDTH, 3 * C_WIDTH], axis=-1)
    logf = jax.nn.log_sigmoid(g.astype(F32) + b_forget.astype(F32))
    return split_heads(q, C_HEADS), split_heads(k, C_HEADS), split_heads(v, C_HEADS), logf


def fox_core(q, k, v, fq, fk, q_pos, k_pos):
    s = jnp.einsum('bqhd,bkhd->bhqk', q, k, preferred_element_type=F32) * (HEAD_DIM ** -0.5)
    s = s + jnp.swapaxes(fq, 1, 2)[..., :, None] - jnp.swapaxes(fk, 1, 2)[..., None, :]
    s = jnp.where(k_pos[None, :] <= q_pos[:, None], s, -jnp.inf)
    p = jax.nn.softmax(s, axis=-1)
    return jnp.einsum('bhqk,bkhd->bqhd', p.astype(v.dtype), v)


def mixer_c_prompt(h, w_in, b_forget, w_out):
    b, t, _ = h.shape
    q, k, v, logf = project_c(h, w_in, b_forget)
    cum = jnp.cumsum(logf, axis=1)
    k_pos = jnp.arange(t)

    def one_block(i):
        start = i * Q_BLOCK
        q_b = lax.dynamic_slice_in_dim(q, start, Q_BLOCK, axis=1)
        f_b = lax.dynamic_slice_in_dim(cum, start, Q_BLOCK, axis=1)
        return fox_core(q_b, k, v, f_b, cum, start + jnp.arange(Q_BLOCK), k_pos)

    out = jnp.moveaxis(lax.map(one_block, jnp.arange(t // Q_BLOCK)), 0, 1).reshape(b, t, C_WIDTH)
    return out @ w_out, k, v, logf


def mixer_c_sample(h, cache_k, cache_v, cache_logf, w_in, b_forget, w_out):
    b, t, _ = h.shape
    past = cache_k.shape[1]
    q, k_new, v_new, logf_new = project_c(h, w_in, b_forget)
    k = jnp.concatenate([cache_k, k_new], axis=1)
    v = jnp.concatenate([cache_v, v_new], axis=1)
    cum = jnp.cumsum(jnp.concatenate([cache_logf.astype(F32), logf_new], axis=1), axis=1)
    out = fox_core(q, k, v, cum[:, past:], cum, past + jnp.arange(t), jnp.arange(past + t))
    return out.reshape(b, t, C_WIDTH) @ w_out, k_new, v_new, logf_new


def setup_inputs(seed: int = 0) -> dict:
    key = jax.random.key(seed)
    keys = jax.random.split(key, 30)

    def nrm(i, shape, scale):
        return jax.random.normal(keys[i], shape, F32) * scale

    la = min(A_REACH, PAST_LEN)
    lb = min(B_REACH, PAST_LEN)
    ws = D_MODEL ** -0.5
    return {
        "x_prompt": nrm(0, (BATCH, SEQ, D_MODEL), 1.0),
        "x_sample": nrm(1, (DEC_BATCH, DEC_SEQ, D_MODEL), 1.0),
        "cache_a_k": nrm(2, (N_EVEN, DEC_BATCH, la, A_HEADS, HEAD_DIM), 1.0),
        "cache_a_v": nrm(3, (N_EVEN, DEC_BATCH, la, A_HEADS, HEAD_DIM), 1.0),
        "cache_b_k": nrm(4, (N_EVEN, DEC_BATCH, lb, B_KV_HEADS, HEAD_DIM), 1.0),
        "cache_b_v": nrm(5, (N_EVEN, DEC_BATCH, lb, B_KV_HEADS, HEAD_DIM), 1.0),
        "cache_c_k": nrm(6, (N_ODD, DEC_BATCH, PAST_LEN, C_HEADS, HEAD_DIM), 1.0),
        "cache_c_v": nrm(7, (N_ODD, DEC_BATCH, PAST_LEN, C_HEADS, HEAD_DIM), 1.0),
        "cache_c_logf": jax.nn.log_sigmoid(FORGET_BIAS + nrm(8, (N_ODD, DEC_BATCH, PAST_LEN, C_HEADS), 1.0)),
        "c_prompt": nrm(9, (BATCH, D_MODEL), 1.0),
        "c_sample": nrm(10, (DEC_BATCH, D_MODEL), 1.0),
        "ada_w": nrm(11, (DEPTH, D_MODEL, 6 * D_MODEL), 0.5 * ws),
        "ada_b": nrm(12, (DEPTH, 6 * D_MODEL), 0.02),
        "norm_g": 1.0 + nrm(13, (DEPTH, 2, D_MODEL), 0.1),
        "w_in_ab": nrm(14, (N_EVEN, D_MODEL, AB_IN), ws),
        "rel_bias_a": nrm(15, (N_EVEN, 2 * REL_CLIP + 1, A_HEADS), 0.5),
        "sinks_b": nrm(16, (N_EVEN, B_HEADS), 1.0),
        "w_out_ab": nrm(17, (N_EVEN, A_WIDTH + B_WIDTH, D_MODEL), (A_WIDTH + B_WIDTH) ** -0.5),
        "ffn_w_gate": nrm(18, (N_EVEN, D_MODEL, FFN_DIM), ws),
        "ffn_w_up": nrm(19, (N_EVEN, D_MODEL, FFN_DIM), ws),
        "ffn_w_down": nrm(20, (N_EVEN, FFN_DIM, D_MODEL), FFN_DIM ** -0.5),
        "w_in_c": nrm(21, (N_ODD, D_MODEL, C_IN), ws),
        "b_forget_c": FORGET_BIAS + nrm(22, (N_ODD, C_HEADS), 0.1),
        "w_out_c": nrm(23, (N_ODD, C_WIDTH, D_MODEL), C_WIDTH ** -0.5),
        "router_w": nrm(24, (N_ODD, D_MODEL, N_EXPERTS), ws),
        "router_b": nrm(25, (N_ODD, N_EXPERTS), 0.01),
        "moe_w_gate": nrm(26, (N_ODD, N_EXPERTS, D_MODEL, EXPERT_DIM), ws),
        "moe_w_up": nrm(27, (N_ODD, N_EXPERTS, D_MODEL, EXPERT_DIM), ws),
        "moe_w_down": nrm(28, (N_ODD, N_EXPERTS, EXPERT_DIM, D_MODEL), EXPERT_DIM ** -0.5),
        "final_norm_g": 1.0 + nrm(29, (D_MODEL,), 0.1),
    }


def reference(x_prompt, x_sample, cache_a_k, cache_a_v, cache_b_k, cache_b_v, cache_c_k, cache_c_v,
              cache_c_logf, c_prompt, c_sample, ada_w, ada_b, norm_g, w_in_ab, rel_bias_a, sinks_b,
              w_out_ab, ffn_w_gate, ffn_w_up, ffn_w_down, w_in_c, b_forget_c, w_out_c, router_w, router_b,
              moe_w_gate, moe_w_up, moe_w_down, final_norm_g):
    past_len = cache_c_k.shape[2]
    xp, xs = x_prompt, x_sample
    even_states = []
    odd_states = []
    for layer in range(DEPTH):
        i = layer // 2
        sh_m_p, sc_m_p, gt_m_p, sh_f_p, sc_f_p, gt_f_p = adaln_params(c_prompt, ada_w[layer], ada_b[layer])
        sh_m_s, sc_m_s, gt_m_s, sh_f_s, sc_f_s, gt_f_s = adaln_params(c_sample, ada_w[layer], ada_b[layer])
        hp = modulate(xp, norm_g[layer, 0], sh_m_p, sc_m_p)
        hs = modulate(xs, norm_g[layer, 0], sh_m_s, sc_m_s)
        if layer % 2 == 0:
            mp, akp, avp, bkp, bvp = mixer_ab_prompt(hp, w_in_ab[i], rel_bias_a[i], sinks_b[i], w_out_ab[i])
            ms, aks, avs, bks, bvs = mixer_ab_sample(hs, cache_a_k[i], cache_a_v[i], cache_b_k[i], cache_b_v[i],
                                                     past_len, w_in_ab[i], rel_bias_a[i], sinks_b[i], w_out_ab[i])
            even_states.append((akp, avp, aks, avs, bkp, bvp, bks, bvs))
        else:
            mp, ckp, cvp, cfp = mixer_c_prompt(hp, w_in_c[i], b_forget_c[i], w_out_c[i])
            ms, cks, cvs, cfs = mixer_c_sample(hs, cache_c_k[i], cache_c_v[i], cache_c_logf[i],
                                               w_in_c[i], b_forget_c[i], w_out_c[i])
            odd_states.append((ckp, cvp, cfp, cks, cvs, cfs))
        xp = xp + gt_m_p * mp
        xs = xs + gt_m_s * ms
        hp = modulate(xp, norm_g[layer, 1], sh_f_p, sc_f_p)
        hs = modulate(xs, norm_g[layer, 1], sh_f_s, sc_f_s)
        if layer % 2 == 0:
            fp = swiglu(hp, ffn_w_gate[i], ffn_w_up[i], ffn_w_down[i])
            fs = swiglu(hs, ffn_w_gate[i], ffn_w_up[i], ffn_w_down[i])
        else:
            fp = moe_swiglu(hp, router_w[i], router_b[i], moe_w_gate[i], moe_w_up[i], moe_w_down[i])
            fs = moe_swiglu(hs, router_w[i], router_b[i], moe_w_gate[i], moe_w_up[i], moe_w_down[i])
        xp = xp + gt_f_p * fp
        xs = xs + gt_f_s * fs
    y_prompt = rms_norm(xp, final_norm_g)
    y_sample = rms_norm(xs, final_norm_g)
    (new_a_k_prompt, new_a_v_prompt, new_a_k_sample, new_a_v_sample,
     new_b_k_prompt, new_b_v_prompt, new_b_k_sample, new_b_v_sample) = [jnp.stack(s) for s in zip(*even_states)]
    (new_c_k_prompt, new_c_v_prompt, new_c_logf_prompt,
     new_c_k_sample, new_c_v_sample, new_c_logf_sample) = [jnp.stack(s) for s in zip(*odd_states)]
    return (y_prompt, y_sample,
            new_a_k_prompt, new_a_v_prompt, new_a_k_sample, new_a_v_sample,
            new_b_k_prompt, new_b_v_prompt, new_b_k_sample, new_b_v_sample,
            new_c_k_prompt, new_c_v_prompt, new_c_logf_prompt,
            new_c_k_sample, new_c_v_sample, new_c_logf_sample)
```

```python
import functools

import numpy as np
import jax
import jax.numpy as jnp
from jax import lax
from jax.experimental import pallas as pl
from jax.experimental.pallas import tpu as pltpu

F32 = jnp.float32
BF16 = jnp.bfloat16

HEAD_DIM = 128
CHUNK = 64
A_PREV_CHUNKS = 8
B_PREV_CHUNKS = 2
TOP_K = 2
EPS = 1e-6
NEG = -1e30
LANES = 128
VMEM_LIMIT = 56 * 1024 * 1024
Q_ROWS = 512


def _pick(n, prefs):
    for p in prefs:
        if p <= n and n % p == 0:
            return p
    return n


def _pallas(body, sem, **kw):
    fn = body.func if isinstance(body, functools.partial) else body
    params = pltpu.CompilerParams(dimension_semantics=sem, vmem_limit_bytes=VMEM_LIMIT)
    return pl.pallas_call(body, name=fn.__name__.strip("_"), compiler_params=params, **kw)


def _rows(p, tm):
    g, n = p.shape
    if g == tm:
        return p
    return jnp.broadcast_to(p[:, None, :], (g, tm // g, n)).reshape(tm, n)


def _dot(a, b):
    return jnp.dot(a, b, preferred_element_type=F32)


def _dot_t(a, b):
    return lax.dot_general(a, b, (((1,), (1,)), ((), ())), preferred_element_type=F32)


def _silu(x):
    return x / (1.0 + jnp.exp(-x))


def _adaln_kernel(c_ref, w_ref, b_ref, o_ref):
    a = _silu(c_ref[...]).astype(BF16)
    o_ref[...] = _dot(a, w_ref[...].astype(BF16)) + b_ref[...]


def _adaln(c, ada_w, ada_b):
    depth, d, n = ada_w.shape
    rows = c.shape[0]
    tn = _pick(n, (512, 256, 128))
    return _pallas(
        _adaln_kernel,
        grid=(depth, n // tn),
        in_specs=[
            pl.BlockSpec((rows, d), lambda l, j: (0, 0)),
            pl.BlockSpec((None, d, tn), lambda l, j: (l, 0, j)),
            pl.BlockSpec((None, 1, tn), lambda l, j: (l, 0, j)),
        ],
        out_specs=pl.BlockSpec((None, rows, tn), lambda l, j: (l, 0, j)),
        out_shape=jax.ShapeDtypeStruct((depth, rows, n), F32),
        sem=("arbitrary", "arbitrary"),
    )(c, ada_w, ada_b.reshape(depth, 1, n))


def _normed(x, g):
    ms = jnp.mean(x * x, axis=-1, keepdims=True)
    return x * lax.rsqrt(ms + EPS) * g


def _modulate_kernel(x_ref, g_ref, sc_ref, sh_ref, o_ref):
    tm = x_ref.shape[0]
    y = _normed(x_ref[...], g_ref[...])
    y = y * (1.0 + _rows(sc_ref[...], tm)) + _rows(sh_ref[...], tm)
    o_ref[...] = y.astype(o_ref.dtype)


def _modulate(x, g, sc, sh, group, out_dtype):
    m, d = x.shape
    tm = _pick(m, (256,))
    gpt = tm // group
    return _pallas(
        _modulate_kernel,
        grid=(m // tm,),
        in_specs=[
            pl.BlockSpec((tm, d), lambda i: (i, 0)),
            pl.BlockSpec((1, d), lambda i: (0, 0)),
            pl.BlockSpec((gpt, d), lambda i: (i, 0)),
            pl.BlockSpec((gpt, d), lambda i: (i, 0)),
        ],
        out_specs=pl.BlockSpec((tm, d), lambda i: (i, 0)),
        out_shape=jax.ShapeDtypeStruct((m, d), out_dtype),
        sem=("arbitrary",),
    )(x, g.reshape(1, d), sc, sh)


def _final_norm_kernel(x_ref, g_ref, o_ref):
    o_ref[...] = _normed(x_ref[...], g_ref[...])


def _final_norm(x, g, row0, rows):
    m, d = x.shape
    tm = _pick(rows, (256,))
    assert row0 % tm == 0
    off = row0 // tm
    return _pallas(
        _final_norm_kernel,
        grid=(rows // tm,),
        in_specs=[
            pl.BlockSpec((tm, d), lambda i: (i + off, 0)),
            pl.BlockSpec((1, d), lambda i: (0, 0)),
        ],
        out_specs=pl.BlockSpec((tm, d), lambda i: (i, 0)),
        out_shape=jax.ShapeDtypeStruct((rows, d), F32),
        sem=("arbitrary",),
    )(x, g.reshape(1, d))


def _route_kernel(x_ref, g_ref, sc_ref, sh_ref, rw_ref, rb_ref, h_ref, ti_ref, tg_ref, *, n_experts):
    tm = x_ref.shape[0]
    y = _normed(x_ref[...], g_ref[...])
    y = y * (1.0 + _rows(sc_ref[...], tm)) + _rows(sh_ref[...], tm)
    h_ref[...] = y
    logits = _dot(y.astype(BF16), rw_ref[...]) + rb_ref[...]
    lane = lax.broadcasted_iota(jnp.int32, logits.shape, 1)
    l1 = jnp.where(lane < n_experts, logits, NEG)
    m1 = jnp.max(l1, axis=-1, keepdims=True)
    i1 = jnp.min(jnp.where(l1 == m1, lane, LANES), axis=-1, keepdims=True)
    l2 = jnp.where(lane == i1, NEG, l1)
    m2 = jnp.max(l2, axis=-1, keepdims=True)
    i2 = jnp.min(jnp.where(l2 == m2, lane, LANES), axis=-1, keepdims=True)
    e2 = jnp.exp(m2 - m1)
    den = 1.0 + e2
    ti_ref[...] = jnp.where(lane == 0, i1, jnp.where(lane == 1, i2, 0))
    tg_ref[...] = jnp.where(lane == 0, 1.0 / den, jnp.where(lane == 1, e2 / den, 0.0))


def _route(x, g, sc, sh, rw, rb, group):
    m, d = x.shape
    n_experts = rw.shape[1]
    tm = _pick(m, (256,))
    gpt = tm // group
    rw_p = jnp.zeros((d, LANES), BF16).at[:, :n_experts].set(rw.astype(BF16))
    rb_p = jnp.zeros((1, LANES), F32).at[0, :n_experts].set(rb.astype(F32))
    row = lambda i: (i, 0)
    fixed = lambda i: (0, 0)
    return _pallas(
        functools.partial(_route_kernel, n_experts=n_experts),
        grid=(m // tm,),
        in_specs=[
            pl.BlockSpec((tm, d), row),
            pl.BlockSpec((1, d), fixed),
            pl.BlockSpec((gpt, d), row),
            pl.BlockSpec((gpt, d), row),
            pl.BlockSpec((d, LANES), fixed),
            pl.BlockSpec((1, LANES), fixed),
        ],
        out_specs=[
            pl.BlockSpec((tm, d), row),
            pl.BlockSpec((tm, LANES), row),
            pl.BlockSpec((tm, LANES), row),
        ],
        out_shape=[
            jax.ShapeDtypeStruct((m, d), F32),
            jax.ShapeDtypeStruct((m, LANES), jnp.int32),
            jax.ShapeDtypeStruct((m, LANES), F32),
        ],
        sem=("arbitrary",),
    )(x, g.reshape(1, d), sc, sh, rw_p, rb_p)


def _mm_kernel(a_ref, w_ref, o_ref):
    o_ref[...] = _dot(a_ref[...], w_ref[...])


def _mm(a, w):
    m, k = a.shape
    n = w.shape[1]
    tm = _pick(m, (512, 256))
    tn = _pick(n, (1024, 512, 256, 128))
    return _pallas(
        _mm_kernel,
        grid=(n // tn, m // tm),
        in_specs=[
            pl.BlockSpec((tm, k), lambda j, i: (i, 0)),
            pl.BlockSpec((k, tn), lambda j, i: (0, j)),
        ],
        out_specs=pl.BlockSpec((tm, tn), lambda j, i: (i, j)),
        out_shape=jax.ShapeDtypeStruct((m, n), F32),
        sem=("arbitrary", "arbitrary"),
    )(a, w)


def _mm_logsig_kernel(a_ref, w_ref, b_ref, o_ref):
    z = _dot(a_ref[...], w_ref[...]) + b_ref[...]
    o_ref[...] = jnp.minimum(z, 0.0) - jnp.log1p(jnp.exp(-jnp.abs(z)))


def _mm_logsig(a, w, b):
    m, k = a.shape
    n = w.shape[1]
    tm = _pick(m, (512, 256))
    return _pallas(
        _mm_logsig_kernel,
        grid=(m // tm,),
        in_specs=[
            pl.BlockSpec((tm, k), lambda i: (i, 0)),
            pl.BlockSpec((k, n), lambda i: (0, 0)),
            pl.BlockSpec((1, n), lambda i: (0, 0)),
        ],
        out_specs=pl.BlockSpec((tm, n), lambda i: (i, 0)),
        out_shape=jax.ShapeDtypeStruct((m, n), F32),
        sem=("arbitrary",),
    )(a, w, b)


def _mm_swiglu_kernel(a_ref, wg_ref, wu_ref, o_ref):
    a = a_ref[...]
    g = _dot(a, wg_ref[...])
    u = _dot(a, wu_ref[...])
    o_ref[...] = (_silu(g) * u).astype(o_ref.dtype)


def _mm_swiglu(a, wg, wu):
    m, k = a.shape
    n = wg.shape[1]
    tm = _pick(m, (512, 256))
    tn = _pick(n, (512, 256, 128))
    return _pallas(
        _mm_swiglu_kernel,
        grid=(n // tn, m // tm),
        in_specs=[
            pl.BlockSpec((tm, k), lambda j, i: (i, 0)),
            pl.BlockSpec((k, tn), lambda j, i: (0, j)),
            pl.BlockSpec((k, tn), lambda j, i: (0, j)),
        ],
        out_specs=pl.BlockSpec((tm, tn), lambda j, i: (i, j)),
        out_shape=jax.ShapeDtypeStruct((m, n), BF16),
        sem=("arbitrary", "arbitrary"),
    )(a, wg, wu)


def _mm_resid_kernel(a_ref, w_ref, x_ref, g_ref, o_ref, *scratch, nk):
    tm = a_ref.shape[0]
    part = _dot(a_ref[...], w_ref[...])
    if nk == 1:
        o_ref[...] = x_ref[...] + _rows(g_ref[...], tm) * part
        return
    acc_ref, = scratch
    k = pl.program_id(2)

    @pl.when(k == 0)
    def _():
        acc_ref[...] = part

    @pl.when(k > 0)
    def _():
        acc_ref[...] += part

    @pl.when(k == nk - 1)
    def _():
        o_ref[...] = x_ref[...] + _rows(g_ref[...], tm) * acc_ref[...]


def _mm_resid(a, w, x, gate, group):
    m, kdim = a.shape
    n = w.shape[1]
    tm = _pick(m, (512, 256))
    tn = _pick(n, (1024, 512, 256, 128))
    tk = kdim if kdim <= 4096 else _pick(kdim, (1024, 512, 256, 128))
    nk = kdim // tk
    gpt = tm // group
    scratch = [] if nk == 1 else [pltpu.VMEM((tm, tn), F32)]
    return _pallas(
        functools.partial(_mm_resid_kernel, nk=nk),
        grid=(n // tn, m // tm, nk),
        in_specs=[
            pl.BlockSpec((tm, tk), lambda j, i, k: (i, k)),
            pl.BlockSpec((tk, tn), lambda j, i, k: (k, j)),
            pl.BlockSpec((tm, tn), lambda j, i, k: (i, j)),
            pl.BlockSpec((gpt, tn), lambda j, i, k: (i, j)),
        ],
        out_specs=pl.BlockSpec((tm, tn), lambda j, i, k: (i, j)),
        out_shape=jax.ShapeDtypeStruct((m, n), F32),
        scratch_shapes=scratch,
        sem=("arbitrary", "arbitrary", "arbitrary"),
    )(a, w, x, gate)


def _band_core(q, k1, v1, k2, v2, b1, b2, sink, first_ok):
    scale = HEAD_DIM ** -0.5
    qb = q.astype(BF16)
    s1 = _dot_t(qb, k1.astype(BF16)) * scale + b1
    s2 = _dot_t(qb, k2.astype(BF16)) * scale + b2
    if first_ok is not None:
        s1 = jnp.where(first_ok, s1, NEG)
    m = jnp.maximum(jnp.max(s1, axis=-1, keepdims=True), jnp.max(s2, axis=-1, keepdims=True))
    m = jnp.maximum(m, sink)
    e1 = jnp.exp(s1 - m)
    e2 = jnp.exp(s2 - m)
    den = jnp.sum(e1, axis=-1, keepdims=True) + jnp.sum(e2, axis=-1, keepdims=True) + jnp.exp(sink - m)
    p1 = (e1 / den).astype(BF16)
    p2 = (e2 / den).astype(BF16)
    return _dot(p1, v1.astype(BF16)) + _dot(p2, v2.astype(BF16))


def _band_prompt_kernel(sink_ref, q_ref, kp_ref, kc_ref, vp_ref, vc_ref, bp_ref, bc_ref, o_in_ref, o_ref):
    del o_in_ref
    h = pl.program_id(0)
    blk = pl.program_id(2)
    out = _band_core(q_ref[...], kp_ref[...], vp_ref[...], kc_ref[...], vc_ref[...],
                     bp_ref[...], bc_ref[...], sink_ref[h], blk > 0)
    o_ref[...] = out.astype(o_ref.dtype)


def _band_prompt(z, bias, sinks, batch, seq, m_total, cols):
    qcol, kcol, vcol = cols
    n_heads = bias.shape[0]
    nb = seq // Q_ROWS
    hd = HEAD_DIM

    def rows(h, b, i, s):
        return b * nb + i

    def prev_rows(h, b, i, s):
        return b * nb + jnp.maximum(i - 1, 0)

    def spec(row_fn, col_fn):
        return pl.BlockSpec((Q_ROWS, hd), lambda h, b, i, s: (row_fn(h, b, i, s), col_fn(h)))

    grid_spec = pltpu.PrefetchScalarGridSpec(
        num_scalar_prefetch=1,
        grid=(n_heads, batch, nb),
        in_specs=[
            spec(rows, qcol),
            spec(prev_rows, kcol),
            spec(rows, kcol),
            spec(prev_rows, vcol),
            spec(rows, vcol),
            pl.BlockSpec((None, Q_ROWS, Q_ROWS), lambda h, b, i, s: (h, 0, 0)),
            pl.BlockSpec((None, Q_ROWS, Q_ROWS), lambda h, b, i, s: (h, 0, 1)),
            pl.BlockSpec(memory_space=pl.ANY),
        ],
        out_specs=pl.BlockSpec((Q_ROWS, hd), lambda h, b, i, s: (b * nb + i, h)),
    )
    o_init = jnp.zeros((m_total, n_heads * hd), BF16)
    return _pallas(
        _band_prompt_kernel,
        grid_spec=grid_spec,
        out_shape=jax.ShapeDtypeStruct(o_init.shape, o_init.dtype),
        input_output_aliases={8: 0},
        sem=("arbitrary", "arbitrary", "arbitrary"),
    )(sinks, z, z, z, z, z, bias, bias, o_init)


def _band_sample_kernel(sink_ref, q_ref, k1_ref, v1_ref, k2_ref, v2_ref, b1_ref, b2_ref, o_in_ref, o_ref):
    del o_in_ref
    h = pl.program_id(0)
    out = _band_core(q_ref[...], k1_ref[...], v1_ref[...], k2_ref[...], v2_ref[...],
                     b1_ref[...], b2_ref[...], sink_ref[h], None)
    o_ref[...] = out.astype(o_ref.dtype)


def _band_sample(z, cache_k, cache_v, bias1, bias2, sinks, o, row0, t_new, cols, kv_group, out_col0):
    qcol, kcol, vcol = cols
    n_heads = bias1.shape[0]
    db, n_cache, _ = cache_k.shape
    hd = HEAD_DIM
    rb0 = row0 // t_new

    new = lambda col_fn: pl.BlockSpec((t_new, hd), lambda h, b, s: (rb0 + b, col_fn(h)))
    cache = pl.BlockSpec((None, n_cache, hd), lambda h, b, s: (b, 0, h // kv_group))
    grid_spec = pltpu.PrefetchScalarGridSpec(
        num_scalar_prefetch=1,
        grid=(n_heads, db),
        in_specs=[
            new(qcol), cache, cache, new(kcol), new(vcol),
            pl.BlockSpec((None, t_new, n_cache), lambda h, b, s: (h, 0, 0)),
            pl.BlockSpec((None, t_new, t_new), lambda h, b, s: (h, 0, 0)),
            pl.BlockSpec(memory_space=pl.ANY),
        ],
        out_specs=pl.BlockSpec((t_new, hd), lambda h, b, s: (rb0 + b, out_col0 + h)),
    )
    return _pallas(
        _band_sample_kernel,
        grid_spec=grid_spec,
        out_shape=jax.ShapeDtypeStruct(o.shape, o.dtype),
        input_output_aliases={8: 0},
        sem=("arbitrary", "arbitrary"),
    )(sinks, z, cache_k, cache_v, z, z, bias1, bias2, o)


def _chunk_band_mask(q_pos, k_pos, n_prev):
    qc = q_pos[:, None] // CHUNK
    kc = k_pos[None, :] // CHUNK
    return (kc <= qc) & (kc >= qc - n_prev) & (k_pos[None, :] >= 0)


def _rel_bias(table, q_pos, k_pos):
    clip = (table.shape[0] - 1) // 2
    nq, nk = len(q_pos), len(k_pos)
    diff = (q_pos[-1] - k_pos[0]) - np.arange(nq + nk - 1)
    vec = table[np.clip(diff, -clip, clip) + clip].astype(F32).T
    starts = jnp.arange(nq - 1, -1, -1)
    window = lambda v, s: lax.dynamic_slice_in_dim(v, s, nk, axis=0)
    return jax.vmap(jax.vmap(window, in_axes=(None, 0)), in_axes=(0, None))(vec, starts)


def _alibi_bias(n_heads, q_pos, k_pos):
    slopes = 2.0 ** (-8.0 * jnp.arange(1, n_heads + 1, dtype=F32) / n_heads)
    dist = jnp.asarray(np.abs(q_pos[:, None] - k_pos[None, :]), F32)
    return -slopes[:, None, None] * dist


def _bias_tables(rel_table, n_b_heads, q_pos, k_pos):
    ma = _chunk_band_mask(q_pos, k_pos, A_PREV_CHUNKS)
    mb = _chunk_band_mask(q_pos, k_pos, B_PREV_CHUNKS)
    ba = jnp.where(ma[None], _rel_bias(rel_table, q_pos, k_pos), NEG)
    bb = jnp.where(mb[None], _alibi_bias(n_b_heads, q_pos, k_pos), NEG)
    return ba, bb


def _cumsum_kernel(*refs, seg_lens, blk):
    n = len(seg_lens)
    in_refs, out_refs = refs[:n], refs[n:]
    carry = jnp.zeros((1, in_refs[0].shape[-1]), F32)
    for x_ref, o_ref, length in zip(in_refs, out_refs, seg_lens):
        for start in range(0, length, blk):
            size = min(blk, length - start)
            r = lax.broadcasted_iota(jnp.int32, (size, size), 0)
            c = lax.broadcasted_iota(jnp.int32, (size, size), 1)
            tri = jnp.where(c <= r, 1.0, 0.0).astype(F32)
            x = x_ref[start:start + size, :]
            cs = jnp.dot(tri, x, preferred_element_type=F32, precision=lax.Precision.HIGHEST) + carry
            o_ref[start:start + size, :] = cs
            carry = cs[size - 1:size, :]


def _cumsum(segs):
    g = segs[0].shape[0]
    seg_lens = tuple(s.shape[1] for s in segs)
    specs = [pl.BlockSpec((None, t, LANES), lambda i: (i, 0, 0)) for t in seg_lens]
    outs = _pallas(
        functools.partial(_cumsum_kernel, seg_lens=seg_lens, blk=256),
        grid=(g,),
        in_specs=specs,
        out_specs=specs,
        out_shape=[jax.ShapeDtypeStruct(s.shape, F32) for s in segs],
        sem=("arbitrary",),
    )(*segs)
    return outs


def _lane_pick(x, h):
    lane = lax.broadcasted_iota(jnp.int32, x.shape, 1)
    return jnp.sum(jnp.where(lane == h, x, 0.0), axis=-1, keepdims=True)


def _fox_prompt_kernel(q_ref, k_ref, v_ref, cq_ref, ck_ref, o_in_ref, o_ref):
    del o_in_ref
    h = pl.program_id(1)
    i = pl.program_id(2)
    tq = q_ref.shape[0]
    scale = HEAD_DIM ** -0.5
    q = q_ref[...].astype(BF16)
    fq = _lane_pick(cq_ref[...], h)

    def block(j, carry, diagonal):
        m, l, acc = carry
        start = pl.multiple_of(j * tq, tq)
        k = k_ref[pl.ds(start, tq), :].astype(BF16)
        v = v_ref[pl.ds(start, tq), :].astype(BF16)
        s = _dot_t(q, k) * scale + fq - ck_ref[pl.ds(j, 1), :]
        if diagonal:
            r = lax.broadcasted_iota(jnp.int32, s.shape, 0)
            c = lax.broadcasted_iota(jnp.int32, s.shape, 1)
            s = jnp.where(c <= r, s, NEG)
        m_new = jnp.maximum(m, jnp.max(s, axis=-1, keepdims=True))
        alpha = jnp.exp(m - m_new)
        p = jnp.exp(s - m_new)
        l = alpha * l + jnp.sum(p, axis=-1, keepdims=True)
        acc = alpha * acc + _dot(p.astype(BF16), v)
        return m_new, l, acc

    init = (jnp.full((tq, 1), NEG, F32), jnp.zeros((tq, 1), F32), jnp.zeros((tq, HEAD_DIM), F32))
    carry = lax.fori_loop(0, i, lambda j, c: block(j, c, False), init)
    _, l, acc = block(i, carry, True)
    o_ref[...] = (acc / l).astype(o_ref.dtype)


def _fox_prompt(zc, cum, cum_t, batch, seq, n_heads, m_total):
    nb = seq // Q_ROWS
    hd = HEAD_DIM
    o_init = jnp.zeros((m_total, n_heads * hd), BF16)
    return _pallas(
        _fox_prompt_kernel,
        grid=(batch, n_heads, nb),
        in_specs=[
            pl.BlockSpec((Q_ROWS, hd), lambda b, h, i: (b * nb + i, h)),
            pl.BlockSpec((seq, hd), lambda b, h, i: (b, n_heads + h)),
            pl.BlockSpec((seq, hd), lambda b, h, i: (b, 2 * n_heads + h)),
            pl.BlockSpec((Q_ROWS, LANES), lambda b, h, i: (b * nb + i, 0)),
            pl.BlockSpec((None, None, nb, Q_ROWS), lambda b, h, i: (b, h, 0, 0)),
            pl.BlockSpec(memory_space=pl.ANY),
        ],
        out_specs=pl.BlockSpec((Q_ROWS, hd), lambda b, h, i: (b * nb + i, h)),
        out_shape=jax.ShapeDtypeStruct(o_init.shape, o_init.dtype),
        input_output_aliases={5: 0},
        sem=("arbitrary", "arbitrary", "arbitrary"),
    )(zc, zc, zc, cum, cum_t, o_init)


def _fox_sample_kernel(q_ref, k1_ref, v1_ref, k2_ref, v2_ref, cq_ref, ck1_ref, ck2_ref, o_in_ref, o_ref):
    del o_in_ref
    h = pl.program_id(1)
    scale = HEAD_DIM ** -0.5
    q = q_ref[...].astype(BF16)
    fq = _lane_pick(cq_ref[...], h)
    s1 = _dot_t(q, k1_ref[...].astype(BF16)) * scale + fq - ck1_ref[...]
    s2 = _dot_t(q, k2_ref[...].astype(BF16)) * scale + fq - ck2_ref[...]
    r = lax.broadcasted_iota(jnp.int32, s2.shape, 0)
    c = lax.broadcasted_iota(jnp.int32, s2.shape, 1)
    s2 = jnp.where(c <= r, s2, NEG)
    m = jnp.maximum(jnp.max(s1, axis=-1, keepdims=True), jnp.max(s2, axis=-1, keepdims=True))
    e1 = jnp.exp(s1 - m)
    e2 = jnp.exp(s2 - m)
    den = jnp.sum(e1, axis=-1, keepdims=True) + jnp.sum(e2, axis=-1, keepdims=True)
    out = _dot((e1 / den).astype(BF16), v1_ref[...].astype(BF16)) + _dot((e2 / den).astype(BF16), v2_ref[...].astype(BF16))
    o_ref[...] = out.astype(o_ref.dtype)


def _fox_sample(zc, cache_k, cache_v, cum_new, cum_t_cache, cum_t_new, o, row0, t_new, n_heads):
    db, past, _ = cache_k.shape
    hd = HEAD_DIM
    rb0 = row0 // t_new
    new = lambda c0: pl.BlockSpec((t_new, hd), lambda b, h: (rb0 + b, c0 + h))
    cache = pl.BlockSpec((None, past, hd), lambda b, h: (b, 0, h))
    return _pallas(
        _fox_sample_kernel,
        grid=(db, n_heads),
        in_specs=[
            new(0), cache, cache, new(n_heads), new(2 * n_heads),
            pl.BlockSpec((t_new, LANES), lambda b, h: (b, 0)),
            pl.BlockSpec((None, None, 1, past), lambda b, h: (b, h, 0, 0)),
            pl.BlockSpec((None, None, 1, t_new), lambda b, h: (b, h, 0, 0)),
            pl.BlockSpec(memory_space=pl.ANY),
        ],
        out_specs=pl.BlockSpec((t_new, hd), lambda b, h: (rb0 + b, h)),
        out_shape=jax.ShapeDtypeStruct(o.shape, o.dtype),
        input_output_aliases={8: 0},
        sem=("arbitrary", "arbitrary"),
    )(zc, cache_k, cache_v, zc, zc, cum_new, cum_t_cache, cum_t_new, o)


def _gather_rows_kernel(idx_ref, h_ref, o_ref, buf_ref, sem):
    rows = o_ref.shape[0]

    def copy(r):
        return pltpu.make_async_copy(h_ref.at[pl.ds(idx_ref[0, 0, r], 1), :], buf_ref.at[pl.ds(r, 1), :], sem)

    def start(r, c):
        copy(r).start()
        return c

    def wait(r, c):
        copy(r).wait()
        return c

    lax.fori_loop(0, rows, start, 0)
    lax.fori_loop(0, rows, wait, 0)
    o_ref[...] = buf_ref[...].astype(o_ref.dtype)


def _gather_rows(h, row_token, tile):
    p = row_token.shape[0]
    d = h.shape[1]
    nt = p // tile
    return _pallas(
        _gather_rows_kernel,
        grid=(nt,),
        in_specs=[
            pl.BlockSpec((1, 1, tile), lambda i: (i, 0, 0), memory_space=pltpu.SMEM),
            pl.BlockSpec(memory_space=pl.ANY),
        ],
        out_specs=pl.BlockSpec((tile, d), lambda i: (i, 0)),
        out_shape=jax.ShapeDtypeStruct((p, d), BF16),
        scratch_shapes=[pltpu.VMEM((tile, d), F32), pltpu.SemaphoreType.DMA(())],
        sem=("arbitrary",),
    )(row_token.reshape(nt, 1, tile), h)


def _moe_gateup_kernel(te_ref, nv_ref, a_ref, wg_ref, wu_ref, o_ref):
    del te_ref
    valid = pl.program_id(1) < nv_ref[0]

    @pl.when(valid)
    def _():
        a = a_ref[...]
        g = _dot(a, wg_ref[...])
        u = _dot(a, wu_ref[...])
        o_ref[...] = (_silu(g) * u).astype(o_ref.dtype)

    @pl.when(jnp.logical_not(valid))
    def _():
        o_ref[...] = jnp.zeros_like(o_ref)


def _moe_gateup(a, wg, wu, tile_expert, n_valid, tile):
    p, d = a.shape
    f = wg.shape[2]
    tn = _pick(f, (512, 256, 128))
    nt = p // tile

    def a_map(j, i, te, nv):
        return (jnp.minimum(i, nv[0] - 1), 0)

    def w_map(j, i, te, nv):
        return (te[jnp.minimum(i, nv[0] - 1)], 0, j)

    def o_map(j, i, te, nv):
        return (i, j)

    grid_spec = pltpu.PrefetchScalarGridSpec(
        num_scalar_prefetch=2,
        grid=(f // tn, nt),
        in_specs=[
            pl.BlockSpec((tile, d), a_map),
            pl.BlockSpec((None, d, tn), w_map),
            pl.BlockSpec((None, d, tn), w_map),
        ],
        out_specs=pl.BlockSpec((tile, tn), o_map),
    )
    return _pallas(
        _moe_gateup_kernel,
        grid_spec=grid_spec,
        out_shape=jax.ShapeDtypeStruct((p, f), BF16),
        sem=("arbitrary", "arbitrary"),
    )(tile_expert, n_valid, a, wg, wu)


def _moe_down_kernel(te_ref, nv_ref, a_ref, w_ref, o_ref, acc_ref, *, nk):
    del te_ref
    k = pl.program_id(2)
    valid = pl.program_id(1) < nv_ref[0]

    @pl.when(valid & (k == 0))
    def _():
        acc_ref[...] = _dot(a_ref[...], w_ref[...])

    @pl.when(valid & (k > 0))
    def _():
        acc_ref[...] += _dot(a_ref[...], w_ref[...])

    @pl.when(valid & (k == nk - 1))
    def _():
        o_ref[...] = acc_ref[...]

    @pl.when(jnp.logical_not(valid) & (k == nk - 1))
    def _():
        o_ref[...] = jnp.zeros_like(o_ref)


def _moe_down(a, w, tile_expert, n_valid, tile):
    p, f = a.shape
    d = w.shape[2]
    tn = _pick(d, (2048, 1024, 512, 256, 128))
    tk = _pick(f, (2048, 1024, 512, 256, 128))
    nk = f // tk
    nt = p // tile

    def clamp(i, k, nv):
        ok = i < nv[0]
        return jnp.where(ok, i, nv[0] - 1), jnp.where(ok, k, nk - 1)

    def a_map(j, i, k, te, nv):
        ie, ke = clamp(i, k, nv)
        return (ie, ke)

    def w_map(j, i, k, te, nv):
        ie, ke = clamp(i, k, nv)
        return (te[ie], ke, j)

    def o_map(j, i, k, te, nv):
        return (i, j)

    grid_spec = pltpu.PrefetchScalarGridSpec(
        num_scalar_prefetch=2,
        grid=(d // tn, nt, nk),
        in_specs=[
            pl.BlockSpec((tile, tk), a_map),
            pl.BlockSpec((None, tk, tn), w_map),
        ],
        out_specs=pl.BlockSpec((tile, tn), o_map),
        scratch_shapes=[pltpu.VMEM((tile, tn), F32)],
    )
    return _pallas(
        functools.partial(_moe_down_kernel, nk=nk),
        grid_spec=grid_spec,
        out_shape=jax.ShapeDtypeStruct((p, d), F32),
        sem=("arbitrary", "arbitrary", "arbitrary"),
    )(tile_expert, n_valid, a, w)


def _combine_kernel(dest_ref, y_ref, tg_ref, x_ref, gt_ref, o_ref, buf_ref, sem):
    tb = x_ref.shape[0]

    def copy(t, k):
        return pltpu.make_async_copy(y_ref.at[pl.ds(dest_ref[0, 0, TOP_K * t + k], 1), :],
                                     buf_ref.at[k, pl.ds(t, 1), :], sem)

    def start(t, c):
        for k in range(TOP_K):
            copy(t, k).start()
        return c

    def wait(t, c):
        for k in range(TOP_K):
            copy(t, k).wait()
        return c

    lax.fori_loop(0, tb, start, 0)
    lax.fori_loop(0, tb, wait, 0)
    tg = tg_ref[...]
    y = tg[:, 0:1] * buf_ref[0] + tg[:, 1:2] * buf_ref[1]
    o_ref[...] = x_ref[...] + _rows(gt_ref[...], tb) * y


def _combine(ys, dest, tg, x, gate, group):
    m, d = x.shape
    tb = _pick(m, (256,))
    nt = m // tb
    gpt = tb // group
    row = lambda i: (i, 0)
    return _pallas(
        _combine_kernel,
        grid=(nt,),
        in_specs=[
            pl.BlockSpec((1, 1, TOP_K * tb), lambda i: (i, 0, 0), memory_space=pltpu.SMEM),
            pl.BlockSpec(memory_space=pl.ANY),
            pl.BlockSpec((tb, LANES), row),
            pl.BlockSpec((tb, d), row),
            pl.BlockSpec((gpt, d), row),
        ],
        out_specs=pl.BlockSpec((tb, d), row),
        out_shape=jax.ShapeDtypeStruct((m, d), F32),
        scratch_shapes=[pltpu.VMEM((TOP_K, tb, d), F32), pltpu.SemaphoreType.DMA(())],
        sem=("arbitrary",),
    )(dest.reshape(nt, 1, TOP_K * tb), ys, tg, x, gate)


def _routing(ti, n_experts, tile):
    m = ti.shape[0]
    n_slots = m * TOP_K
    e_flat = ti[:, :TOP_K].reshape(n_slots)
    onehot = (e_flat[:, None] == jnp.arange(n_experts, dtype=jnp.int32)[None, :]).astype(jnp.int32)
    csum = jnp.cumsum(onehot, axis=0)
    rank = jnp.sum((csum - onehot) * onehot, axis=1)
    counts = csum[-1]
    tiles_e = (counts + tile - 1) // tile
    padded = tiles_e * tile
    start_pad = jnp.cumsum(padded) - padded
    start_raw = jnp.cumsum(counts) - counts
    dest = start_pad[e_flat] + rank
    nt = n_slots // tile + n_experts
    p = nt * tile
    tile_end = jnp.cumsum(tiles_e)
    n_valid = tile_end[-1:].astype(jnp.int32)
    tile_expert = jnp.minimum(
        jnp.sum((jnp.arange(nt, dtype=jnp.int32)[:, None] >= tile_end[None, :]).astype(jnp.int32), axis=1),
        n_experts - 1).astype(jnp.int32)
    slot_token = jnp.arange(n_slots, dtype=jnp.int32) // TOP_K
    _, sorted_token = lax.sort_key_val(e_flat, slot_token, is_stable=True)
    src = jnp.concatenate([sorted_token, jnp.zeros((n_slots,), jnp.int32)])
    row_token = jnp.zeros((p + n_slots,), jnp.int32)
    for e in range(n_experts):
        seg = lax.dynamic_slice(src, (start_raw[e],), (n_slots,))
        row_token = lax.dynamic_update_slice(row_token, seg, (start_pad[e],))
    return dest.astype(jnp.int32), row_token[:p], tile_expert, n_valid


def kernel(x_prompt, x_sample, cache_a_k, cache_a_v, cache_b_k, cache_b_v, cache_c_k, cache_c_v, cache_c_logf, c_prompt, c_sample, ada_w, ada_b, norm_g, w_in_ab, rel_bias_a, sinks_b, w_out_ab, ffn_w_gate, ffn_w_up, ffn_w_down, w_in_c, b_forget_c, w_out_c, router_w, router_b, moe_w_gate, moe_w_up, moe_w_down, final_norm_g):
    batch, seq, d = x_prompt.shape
    db, t_new, _ = x_sample.shape
    depth = ada_w.shape[0]
    ha = cache_a_k.shape[3]
    hkv_b = cache_b_k.shape[3]
    hb = sinks_b.shape[1]
    hc = cache_c_k.shape[3]
    past = cache_c_k.shape[2]
    la = cache_a_k.shape[2]
    lb = cache_b_k.shape[2]
    n_experts = router_w.shape[2]
    hd = HEAD_DIM
    tp = batch * seq
    ts = db * t_new
    m = tp + ts
    group = t_new
    assert seq % Q_ROWS == 0 and Q_ROWS % CHUNK == 0 and A_PREV_CHUNKS * CHUNK <= Q_ROWS
    assert group % 8 == 0 and seq % group == 0 and t_new <= CHUNK and past % CHUNK == 0
    assert hc <= LANES and n_experts <= LANES and tp % Q_ROWS == 0

    x = jnp.concatenate([x_prompt.reshape(tp, d), x_sample.reshape(ts, d)], axis=0)

    n_req = batch + db
    req_pad = -(-n_req // 8) * 8
    c_all = jnp.zeros((req_pad, d), F32).at[:n_req].set(jnp.concatenate([c_prompt, c_sample], axis=0))
    mod = _adaln(c_all, ada_w, ada_b)
    group_req = np.concatenate([np.repeat(np.arange(batch), seq // group), batch + np.arange(db)])

    def mod_params(layer):
        per_group = mod[layer][group_req]
        return [per_group[:, i * d:(i + 1) * d] for i in range(6)]

    states_even, states_odd = [], []
    for layer in range(depth):
        i = layer // 2
        sh_m, sc_m, gt_m, sh_f, sc_f, gt_f = mod_params(layer)
        h = _modulate(x, norm_g[layer, 0], sc_m, sh_m, group, BF16)
        if layer % 2 == 0:
            z = _mm(h, w_in_ab[i].astype(BF16))
            ca, cb = ha, 3 * ha
            n_heads = ha + hb
            kvg = hb // hkv_b
            qcol = lambda hh: jnp.where(hh < ha, hh, cb + hh - ha)
            kcol = lambda hh: jnp.where(hh < ha, ca + hh, cb + hb + (hh - ha) // kvg)
            vcol = lambda hh: jnp.where(hh < ha, 2 * ca + hh, cb + hb + hkv_b + (hh - ha) // kvg)
            q_pos = Q_ROWS + np.arange(Q_ROWS)
            k_pos = np.arange(2 * Q_ROWS)
            ba, bb = _bias_tables(rel_bias_a[i], hb, q_pos, k_pos)
            sinks = jnp.concatenate([jnp.full((ha,), NEG, F32), sinks_b[i].astype(F32)])
            o = _band_prompt(z, jnp.concatenate([ba, bb], axis=0), sinks, batch, seq, m, (qcol, kcol, vcol))
            q_pos = past + np.arange(t_new)
            sa, _ = _bias_tables(rel_bias_a[i], hb, q_pos, past - la + np.arange(la + t_new))
            _, sb = _bias_tables(rel_bias_a[i], hb, q_pos, past - lb + np.arange(lb + t_new))
            o = _band_sample(z, cache_a_k[i].reshape(db, la, ha * hd), cache_a_v[i].reshape(db, la, ha * hd),
                             sa[:, :, :la], sa[:, :, la:], jnp.full((ha,), NEG, F32), o, tp, t_new,
                             (lambda hh: hh, lambda hh: ca + hh, lambda hh: 2 * ca + hh), 1, 0)
            o = _band_sample(z, cache_b_k[i].reshape(db, lb, hkv_b * hd), cache_b_v[i].reshape(db, lb, hkv_b * hd),
                             sb[:, :, :lb], sb[:, :, lb:], sinks_b[i].astype(F32), o, tp, t_new,
                             (lambda hh: cb + hh, lambda hh: cb + hb + hh // kvg, lambda hh: cb + hb + hkv_b + hh // kvg),
                             kvg, ha)
            x = _mm_resid(o, w_out_ab[i].astype(BF16), x, gt_m, group)

            zp = z[:tp].reshape(batch, seq, -1)
            zs = z[tp:].reshape(db, t_new, -1)
            wa, wb = ha * hd, hb * hd
            wkv = hkv_b * hd
            ka_p, va_p = zp[:, :, wa:2 * wa], zp[:, :, 2 * wa:3 * wa]
            kb_p, vb_p = zp[:, :, 3 * wa + wb:3 * wa + wb + wkv], zp[:, :, 3 * wa + wb + wkv:]
            ka_s, va_s = zs[:, :, wa:2 * wa], zs[:, :, 2 * wa:3 * wa]
            kb_s, vb_s = zs[:, :, 3 * wa + wb:3 * wa + wb + wkv], zs[:, :, 3 * wa + wb + wkv:]
            lap, lbp = min(la, seq), min(lb, seq)
            heads = lambda t, nh: t.reshape(t.shape[0], t.shape[1], nh, hd)
            roll = lambda cache, new, nh: jnp.concatenate([cache, heads(new, nh)], axis=1)[:, -cache.shape[1]:]
            states_even.append((
                heads(ka_p[:, -lap:], ha), heads(va_p[:, -lap:], ha),
                roll(cache_a_k[i], ka_s, ha), roll(cache_a_v[i], va_s, ha),
                heads(kb_p[:, -lbp:], hkv_b), heads(vb_p[:, -lbp:], hkv_b),
                roll(cache_b_k[i], kb_s, hkv_b), roll(cache_b_v[i], vb_s, hkv_b)))

            h = _modulate(x, norm_g[layer, 1], sc_f, sh_f, group, BF16)
            f = ffn_w_gate.shape[2]
            fpad = -(-f // 1024) * 1024 if f > 1024 else f
            padc = lambda w: jnp.pad(w.astype(BF16), ((0, 0), (0, fpad - f)))
            mid = _mm_swiglu(h, padc(ffn_w_gate[i]), padc(ffn_w_up[i]))
            wd = jnp.pad(ffn_w_down[i].astype(BF16), ((0, fpad - f), (0, 0)))
            x = _mm_resid(mid, wd, x, gt_f, group)
        else:
            wc = hc * hd
            zc = _mm(h, w_in_c[i][:, :3 * wc].astype(BF16))
            wgate = jnp.zeros((d, LANES), BF16).at[:, :hc].set(w_in_c[i][:, 3 * wc:].astype(BF16))
            bgate = jnp.zeros((1, LANES), F32).at[0, :hc].set(b_forget_c[i].astype(F32))
            logf = _mm_logsig(h, wgate, bgate)
            logf_p = logf[:tp].reshape(batch, seq, LANES)
            logf_s = logf[tp:].reshape(db, t_new, LANES)
            cum_p, = _cumsum([logf_p])
            cache_lf = jnp.pad(cache_c_logf[i].astype(F32), ((0, 0), (0, 0), (0, LANES - hc)))
            cum_c, cum_s = _cumsum([cache_lf, logf_s])
            nb = seq // Q_ROWS
            cum_t_p = jnp.swapaxes(cum_p[:, :, :hc], 1, 2).reshape(batch, hc, nb, Q_ROWS)
            o = _fox_prompt(zc, cum_p.reshape(tp, LANES), cum_t_p, batch, seq, hc, m)
            cum_t_c = jnp.swapaxes(cum_c[:, :, :hc], 1, 2).reshape(db, hc, 1, past)
            cum_t_s = jnp.swapaxes(cum_s[:, :, :hc], 1, 2).reshape(db, hc, 1, t_new)
            o = _fox_sample(zc, cache_c_k[i].reshape(db, past, wc), cache_c_v[i].reshape(db, past, wc),
                            cum_s.reshape(ts, LANES), cum_t_c, cum_t_s, o, tp, t_new, hc)
            x = _mm_resid(o, w_out_c[i].astype(BF16), x, gt_m, group)

            zp = zc[:tp].reshape(batch, seq, 3, hc, hd)
            zs = zc[tp:].reshape(db, t_new, 3, hc, hd)
            states_odd.append((zp[:, :, 1], zp[:, :, 2], logf_p[:, :, :hc],
                               zs[:, :, 1], zs[:, :, 2], logf_s[:, :, :hc]))

            h32, ti, tg = _route(x, norm_g[layer, 1], sc_f, sh_f, router_w[i], router_b[i], group)
            tile = _pick(m * TOP_K, (512, 256))
            dest, row_token, tile_expert, n_valid = _routing(ti, n_experts, tile)
            hs = _gather_rows(h32, row_token, tile)
            mid = _moe_gateup(hs, moe_w_gate[i].astype(BF16), moe_w_up[i].astype(BF16), tile_expert, n_valid, tile)
            ys = _moe_down(mid, moe_w_down[i].astype(BF16), tile_expert, n_valid, tile)
            x = _combine(ys, dest, tg, x, gt_f, group)

    y_prompt = _final_norm(x, final_norm_g, 0, tp).reshape(batch, seq, d)
    y_sample = _final_norm(x, final_norm_g, tp, ts).reshape(db, t_new, d)
    even = [jnp.stack(s) for s in zip(*states_even)]
    odd = [jnp.stack(s) for s in zip(*states_odd)]
    return (y_prompt, y_sample, *even, *odd)
```

```python
import functools

import numpy as np
import jax
import jax.numpy as jnp
from jax import lax
from jax.experimental import pallas as pl
from jax.experimental.pallas import tpu as pltpu

F32 = jnp.float32
BF16 = jnp.bfloat16

HEAD_DIM = 128
CHUNK = 64
A_PREV_CHUNKS = 8
B_PREV_CHUNKS = 2
TOP_K = 2
EPS = 1e-6
NEG = -1e30
LANES = 128
VMEM_LIMIT = 56 * 1024 * 1024
Q_ROWS = 512


def _pick(n, prefs):
    for p in prefs:
        if p <= n and n % p == 0:
            return p
    return n


def _pallas(body, sem, **kw):
    fn = body.func if isinstance(body, functools.partial) else body
    params = pltpu.CompilerParams(dimension_semantics=sem, vmem_limit_bytes=VMEM_LIMIT)
    return pl.pallas_call(body, name=fn.__name__.strip("_"), compiler_params=params, **kw)


def _rows(p, tm):
    g, n = p.shape
    if g == tm:
        return p
    return jnp.broadcast_to(p[:, None, :], (g, tm // g, n)).reshape(tm, n)


def _dot(a, b):
    return jnp.dot(a, b, preferred_element_type=F32)


def _dot_t(a, b):
    return lax.dot_general(a, b, (((1,), (1,)), ((), ())), preferred_element_type=F32)


def _silu(x):
    return x / (1.0 + jnp.exp(-x))


def _adaln_kernel(c_ref, w_ref, b_ref, o_ref):
    a = _silu(c_ref[...]).astype(BF16)
    o_ref[...] = _dot(a, w_ref[...].astype(BF16)) + b_ref[...]


def _adaln(c, ada_w, ada_b):
    depth, d, n = ada_w.shape
    rows = c.shape[0]
    tn = _pick(n, (512, 256, 128))
    return _pallas(
        _adaln_kernel,
        grid=(depth, n // tn),
        in_specs=[
            pl.BlockSpec((rows, d), lambda l, j: (0, 0)),
            pl.BlockSpec((None, d, tn), lambda l, j: (l, 0, j)),
            pl.BlockSpec((None, 1, tn), lambda l, j: (l, 0, j)),
        ],
        out_specs=pl.BlockSpec((None, rows, tn), lambda l, j: (l, 0, j)),
        out_shape=jax.ShapeDtypeStruct((depth, rows, n), F32),
        sem=("arbitrary", "arbitrary"),
    )(c, ada_w, ada_b.reshape(depth, 1, n))


def _normed(x, g):
    ms = jnp.mean(x * x, axis=-1, keepdims=True)
    return x * lax.rsqrt(ms + EPS) * g


def _modulate_kernel(x_ref, g_ref, sc_ref, sh_ref, o_ref):
    tm = x_ref.shape[0]
    y = _normed(x_ref[...], g_ref[...])
    y = y * (1.0 + _rows(sc_ref[...], tm)) + _rows(sh_ref[...], tm)
    o_ref[...] = y.astype(o_ref.dtype)


def _modulate(x, g, sc, sh, group, out_dtype):
    m, d = x.shape
    tm = _pick(m, (256,))
    gpt = tm // group
    return _pallas(
        _modulate_kernel,
        grid=(m // tm,),
        in_specs=[
            pl.BlockSpec((tm, d), lambda i: (i, 0)),
            pl.BlockSpec((1, d), lambda i: (0, 0)),
            pl.BlockSpec((gpt, d), lambda i: (i, 0)),
            pl.BlockSpec((gpt, d), lambda i: (i, 0)),
        ],
        out_specs=pl.BlockSpec((tm, d), lambda i: (i, 0)),
        out_shape=jax.ShapeDtypeStruct((m, d), out_dtype),
        sem=("arbitrary",),
    )(x, g.reshape(1, d), sc, sh)


def _final_norm_kernel(x_ref, g_ref, o_ref):
    o_ref[...] = _normed(x_ref[...], g_ref[...])


def _final_norm(x, g, row0, rows):
    m, d = x.shape
    tm = _pick(rows, (256,))
    assert row0 % tm == 0
    off = row0 // tm
    return _pallas(
        _final_norm_kernel,
        grid=(rows // tm,),
        in_specs=[
            pl.BlockSpec((tm, d), lambda i: (i + off, 0)),
            pl.BlockSpec((1, d), lambda i: (0, 0)),
        ],
        out_specs=pl.BlockSpec((tm, d), lambda i: (i, 0)),
        out_shape=jax.ShapeDtypeStruct((rows, d), F32),
        sem=("arbitrary",),
    )(x, g.reshape(1, d))


def _route_kernel(x_ref, g_ref, sc_ref, sh_ref, rw_ref, rb_ref, h_ref, ti_ref, tg_ref, *, n_experts):
    tm = x_ref.shape[0]
    y = _normed(x_ref[...], g_ref[...])
    y = y * (1.0 + _rows(sc_ref[...], tm)) + _rows(sh_ref[...], tm)
    h_ref[...] = y
    logits = _dot(y.astype(BF16), rw_ref[...]) + rb_ref[...]
    lane = lax.broadcasted_iota(jnp.int32, logits.shape, 1)
    l1 = jnp.where(lane < n_experts, logits, NEG)
    m1 = jnp.max(l1, axis=-1, keepdims=True)
    i1 = jnp.min(jnp.where(l1 == m1, lane, LANES), axis=-1, keepdims=True)
    l2 = jnp.where(lane == i1, NEG, l1)
    m2 = jnp.max(l2, axis=-1, keepdims=True)
    i2 = jnp.min(jnp.where(l2 == m2, lane, LANES), axis=-1, keepdims=True)
    e2 = jnp.exp(m2 - m1)
    den = 1.0 + e2
    ti_ref[...] = jnp.where(lane == 0, i1, jnp.where(lane == 1, i2, 0))
    tg_ref[...] = jnp.where(lane == 0, 1.0 / den, jnp.where(lane == 1, e2 / den, 0.0))


def _route(x, g, sc, sh, rw, rb, group):
    m, d = x.shape
    n_experts = rw.shape[1]
    tm = _pick(m, (256,))
    gpt = tm // group
    rw_p = jnp.zeros((d, LANES), BF16).at[:, :n_experts].set(rw.astype(BF16))
    rb_p = jnp.zeros((1, LANES), F32).at[0, :n_experts].set(rb.astype(F32))
    row = lambda i: (i, 0)
    fixed = lambda i: (0, 0)
    return _pallas(
        functools.partial(_route_kernel, n_experts=n_experts),
        grid=(m // tm,),
        in_specs=[
            pl.BlockSpec((tm, d), row),
            pl.BlockSpec((1, d), fixed),
            pl.BlockSpec((gpt, d), row),
            pl.BlockSpec((gpt, d), row),
            pl.BlockSpec((d, LANES), fixed),
            pl.BlockSpec((1, LANES), fixed),
        ],
        out_specs=[
            pl.BlockSpec((tm, d), row),
            pl.BlockSpec((tm, LANES), row),
            pl.BlockSpec((tm, LANES), row),
        ],
        out_shape=[
            jax.ShapeDtypeStruct((m, d), F32),
            jax.ShapeDtypeStruct((m, LANES), jnp.int32),
            jax.ShapeDtypeStruct((m, LANES), F32),
        ],
        sem=("arbitrary",),
    )(x, g.reshape(1, d), sc, sh, rw_p, rb_p)


def _mm_kernel(a_ref, w_ref, o_ref):
    o_ref[...] = _dot(a_ref[...], w_ref[...].astype(BF16))


def _mm(a, w, row0, rows, col0, cols):
    k = a.shape[1]
    tm = _pick(rows, (512, 256))
    tn = _pick(cols, (1024, 512, 256, 128))
    assert row0 % tm == 0 and col0 % tn == 0
    ri, cj = row0 // tm, col0 // tn
    return _pallas(
        _mm_kernel,
        grid=(cols // tn, rows // tm),
        in_specs=[
            pl.BlockSpec((tm, k), lambda j, i: (i + ri, 0)),
            pl.BlockSpec((k, tn), lambda j, i: (0, j + cj)),
        ],
        out_specs=pl.BlockSpec((tm, tn), lambda j, i: (i, j)),
        out_shape=jax.ShapeDtypeStruct((rows, cols), F32),
        sem=("arbitrary", "arbitrary"),
    )(a, w)


def _mm_logsig_kernel(a_ref, w_ref, b_ref, o_ref):
    z = _dot(a_ref[...], w_ref[...]) + b_ref[...]
    o_ref[...] = jnp.minimum(z, 0.0) - jnp.log1p(jnp.exp(-jnp.abs(z)))


def _mm_logsig(a, w, b):
    m, k = a.shape
    n = w.shape[1]
    tm = _pick(m, (512, 256))
    return _pallas(
        _mm_logsig_kernel,
        grid=(m // tm,),
        in_specs=[
            pl.BlockSpec((tm, k), lambda i: (i, 0)),
            pl.BlockSpec((k, n), lambda i: (0, 0)),
            pl.BlockSpec((1, n), lambda i: (0, 0)),
        ],
        out_specs=pl.BlockSpec((tm, n), lambda i: (i, 0)),
        out_shape=jax.ShapeDtypeStruct((m, n), F32),
        sem=("arbitrary",),
    )(a, w, b)


def _mm_swiglu_kernel(a_ref, wg_ref, wu_ref, o_ref):
    a = a_ref[...]
    g = _dot(a, wg_ref[...].astype(BF16))
    u = _dot(a, wu_ref[...].astype(BF16))
    o_ref[...] = (_silu(g) * u).astype(o_ref.dtype)


def _mm_swiglu(a, wg, wu):
    m, k = a.shape
    n = wg.shape[1]
    tm = _pick(m, (512, 256))
    tn = _pick(n, (512, 256, 128))
    return _pallas(
        _mm_swiglu_kernel,
        grid=(n // tn, m // tm),
        in_specs=[
            pl.BlockSpec((tm, k), lambda j, i: (i, 0)),
            pl.BlockSpec((k, tn), lambda j, i: (0, j)),
            pl.BlockSpec((k, tn), lambda j, i: (0, j)),
        ],
        out_specs=pl.BlockSpec((tm, tn), lambda j, i: (i, j)),
        out_shape=jax.ShapeDtypeStruct((m, n), BF16),
        sem=("arbitrary", "arbitrary"),
    )(a, wg, wu)


def _mm_resid_kernel(a_ref, w_ref, x_ref, g_ref, o_ref, *scratch, nk):
    tm = a_ref.shape[0]
    part = _dot(a_ref[...], w_ref[...].astype(BF16))
    if nk == 1:
        o_ref[...] = x_ref[...] + _rows(g_ref[...], tm) * part
        return
    acc_ref, = scratch
    k = pl.program_id(2)

    @pl.when(k == 0)
    def _():
        acc_ref[...] = part

    @pl.when(k > 0)
    def _():
        acc_ref[...] += part

    @pl.when(k == nk - 1)
    def _():
        o_ref[...] = x_ref[...] + _rows(g_ref[...], tm) * acc_ref[...]


def _mm_resid(a, w, x, gate, group):
    m, kdim = a.shape
    n = w.shape[1]
    tk = kdim if kdim <= 4096 else _pick(kdim, (1024, 512, 256, 128))
    nk = kdim // tk
    tm = _pick(m, (1536, 512, 256)) if nk > 1 else _pick(m, (512, 256))
    tn = _pick(n, (1024, 512, 256, 128))
    gpt = tm // group
    scratch = [] if nk == 1 else [pltpu.VMEM((tm, tn), F32)]
    return _pallas(
        functools.partial(_mm_resid_kernel, nk=nk),
        grid=(n // tn, m // tm, nk),
        in_specs=[
            pl.BlockSpec((tm, tk), lambda j, i, k: (i, k)),
            pl.BlockSpec((tk, tn), lambda j, i, k: (k, j)),
            pl.BlockSpec((tm, tn), lambda j, i, k: (i, j)),
            pl.BlockSpec((gpt, tn), lambda j, i, k: (i, j)),
        ],
        out_specs=pl.BlockSpec((tm, tn), lambda j, i, k: (i, j)),
        out_shape=jax.ShapeDtypeStruct((m, n), F32),
        scratch_shapes=scratch,
        sem=("arbitrary", "arbitrary", "arbitrary"),
    )(a, w, x, gate)


def _band_core(q, k1, v1, k2, v2, b1, b2, sink, first_ok):
    scale = HEAD_DIM ** -0.5
    qb = q.astype(BF16)
    s1 = _dot_t(qb, k1.astype(BF16)) * scale + b1
    s2 = _dot_t(qb, k2.astype(BF16)) * scale + b2
    if first_ok is not None:
        s1 = jnp.where(first_ok, s1, NEG)
    m = jnp.maximum(jnp.max(s1, axis=-1, keepdims=True), jnp.max(s2, axis=-1, keepdims=True))
    m = jnp.maximum(m, sink)
    e1 = jnp.exp(s1 - m)
    e2 = jnp.exp(s2 - m)
    den = jnp.sum(e1, axis=-1, keepdims=True) + jnp.sum(e2, axis=-1, keepdims=True) + jnp.exp(sink - m)
    out = _dot(e1.astype(BF16), v1.astype(BF16)) + _dot(e2.astype(BF16), v2.astype(BF16))
    return out * (1.0 / den)


HEADS_PER_STEP = 2


def _band_prompt_kernel(sink_ref, q_ref, *refs):
    hp = HEADS_PER_STEP
    kv = refs[:4 * hp]
    bp_ref, bc_ref, _, o_ref = refs[4 * hp:]
    g = pl.program_id(0)
    blk = pl.program_id(2)
    outs = []
    for t in range(hp):
        kp_ref, kc_ref, vp_ref, vc_ref = kv[4 * t:4 * t + 4]
        q = q_ref[:, t * HEAD_DIM:(t + 1) * HEAD_DIM]
        outs.append(_band_core(q, kp_ref[...], vp_ref[...], kc_ref[...], vc_ref[...],
                               bp_ref[t], bc_ref[t], sink_ref[g * hp + t], blk > 0))
    o_ref[...] = jnp.concatenate(outs, axis=1).astype(o_ref.dtype)


def _band_prompt(z, bias, sinks, batch, seq, m_total, cols):
    qcol, kcol, vcol = cols
    hp = HEADS_PER_STEP
    n_heads = bias.shape[0]
    assert n_heads % hp == 0
    nb = seq // Q_ROWS
    hd = HEAD_DIM

    def rows(b, i):
        return b * nb + i

    def prev_rows(b, i):
        return b * nb + jnp.maximum(i - 1, 0)

    def spec(row_fn, col_fn, t):
        return pl.BlockSpec((Q_ROWS, hd), lambda g, b, i, s: (row_fn(b, i), col_fn(g * hp + t)))

    kv_specs = []
    for t in range(hp):
        kv_specs += [spec(prev_rows, kcol, t), spec(rows, kcol, t), spec(prev_rows, vcol, t), spec(rows, vcol, t)]
    grid_spec = pltpu.PrefetchScalarGridSpec(
        num_scalar_prefetch=1,
        grid=(n_heads // hp, batch, nb),
        in_specs=[
            pl.BlockSpec((Q_ROWS, hp * hd), lambda g, b, i, s: (rows(b, i), qcol(g * hp) // hp)),
            *kv_specs,
            pl.BlockSpec((hp, Q_ROWS, Q_ROWS), lambda g, b, i, s: (g, 0, 0)),
            pl.BlockSpec((hp, Q_ROWS, Q_ROWS), lambda g, b, i, s: (g, 0, 1)),
            pl.BlockSpec(memory_space=pl.ANY),
        ],
        out_specs=pl.BlockSpec((Q_ROWS, hp * hd), lambda g, b, i, s: (rows(b, i), g)),
    )
    o_init = jnp.zeros((m_total, n_heads * hd), BF16)
    n_in = 1 + 1 + 4 * hp + 2
    return _pallas(
        _band_prompt_kernel,
        grid_spec=grid_spec,
        out_shape=jax.ShapeDtypeStruct(o_init.shape, o_init.dtype),
        input_output_aliases={n_in: 0},
        sem=("arbitrary", "arbitrary", "arbitrary"),
    )(sinks, z, *([z] * (4 * hp)), bias, bias, o_init)


def _band_sample_kernel(sink_ref, q_ref, k1_ref, v1_ref, k2_ref, v2_ref, b1_ref, b2_ref, o_in_ref, o_ref):
    del o_in_ref
    h = pl.program_id(0)
    out = _band_core(q_ref[...], k1_ref[...], v1_ref[...], k2_ref[...], v2_ref[...],
                     b1_ref[...], b2_ref[...], sink_ref[h], None)
    o_ref[...] = out.astype(o_ref.dtype)


def _band_sample(z, cache_k, cache_v, bias1, bias2, sinks, o, row0, t_new, cols, kv_group, out_col0):
    qcol, kcol, vcol = cols
    n_heads = bias1.shape[0]
    db, n_cache, _ = cache_k.shape
    hd = HEAD_DIM
    rb0 = row0 // t_new

    new = lambda col_fn: pl.BlockSpec((t_new, hd), lambda h, b, s: (rb0 + b, col_fn(h)))
    cache = pl.BlockSpec((None, n_cache, hd), lambda h, b, s: (b, 0, h // kv_group))
    grid_spec = pltpu.PrefetchScalarGridSpec(
        num_scalar_prefetch=1,
        grid=(n_heads, db),
        in_specs=[
            new(qcol), cache, cache, new(kcol), new(vcol),
            pl.BlockSpec((None, t_new, n_cache), lambda h, b, s: (h, 0, 0)),
            pl.BlockSpec((None, t_new, t_new), lambda h, b, s: (h, 0, 0)),
            pl.BlockSpec(memory_space=pl.ANY),
        ],
        out_specs=pl.BlockSpec((t_new, hd), lambda h, b, s: (rb0 + b, out_col0 + h)),
    )
    return _pallas(
        _band_sample_kernel,
        grid_spec=grid_spec,
        out_shape=jax.ShapeDtypeStruct(o.shape, o.dtype),
        input_output_aliases={8: 0},
        sem=("arbitrary", "arbitrary"),
    )(sinks, z, cache_k, cache_v, z, z, bias1, bias2, o)


def _chunk_band_mask(q_pos, k_pos, n_prev):
    qc = q_pos[:, None] // CHUNK
    kc = k_pos[None, :] // CHUNK
    return (kc <= qc) & (kc >= qc - n_prev) & (k_pos[None, :] >= 0)


def _rel_bias(table, q_pos, k_pos):
    clip = (table.shape[0] - 1) // 2
    nq, nk = len(q_pos), len(k_pos)
    diff = (q_pos[-1] - k_pos[0]) - np.arange(nq + nk - 1)
    vec = table[np.clip(diff, -clip, clip) + clip].astype(F32).T
    n_heads, length = vec.shape
    padded = jnp.concatenate([vec, jnp.zeros((n_heads, 1), F32)], axis=1)
    g = jnp.tile(padded, (1, nq))[:, :nq * length].reshape(n_heads, nq, length)
    return g[:, :, nq - 1:nq - 1 + nk]


def _alibi_bias(n_heads, q_pos, k_pos):
    slopes = 2.0 ** (-8.0 * jnp.arange(1, n_heads + 1, dtype=F32) / n_heads)
    dist = jnp.asarray(np.abs(q_pos[:, None] - k_pos[None, :]), F32)
    return -slopes[:, None, None] * dist


def _bias_tables(rel_table, n_b_heads, q_pos, k_pos):
    ma = _chunk_band_mask(q_pos, k_pos, A_PREV_CHUNKS)
    mb = _chunk_band_mask(q_pos, k_pos, B_PREV_CHUNKS)
    ba = jnp.where(ma[None], _rel_bias(rel_table, q_pos, k_pos), NEG)
    bb = jnp.where(mb[None], _alibi_bias(n_b_heads, q_pos, k_pos), NEG)
    return ba, bb


def _cumsum_kernel(*refs, seg_lens, blk):
    n = len(seg_lens)
    in_refs, out_refs = refs[:n], refs[n:]
    carry = jnp.zeros((1, in_refs[0].shape[-1]), F32)
    for x_ref, o_ref, length in zip(in_refs, out_refs, seg_lens):
        for start in range(0, length, blk):
            size = min(blk, length - start)
            r = lax.broadcasted_iota(jnp.int32, (size, size), 0)
            c = lax.broadcasted_iota(jnp.int32, (size, size), 1)
            tri = jnp.where(c <= r, 1.0, 0.0).astype(F32)
            x = x_ref[start:start + size, :]
            cs = jnp.dot(tri, x, preferred_element_type=F32, precision=lax.Precision.HIGHEST) + carry
            o_ref[start:start + size, :] = cs
            carry = cs[size - 1:size, :]


def _cumsum(segs):
    g = segs[0].shape[0]
    seg_lens = tuple(s.shape[1] for s in segs)
    specs = [pl.BlockSpec((None, t, LANES), lambda i: (i, 0, 0)) for t in seg_lens]
    outs = _pallas(
        functools.partial(_cumsum_kernel, seg_lens=seg_lens, blk=256),
        grid=(g,),
        in_specs=specs,
        out_specs=specs,
        out_shape=[jax.ShapeDtypeStruct(s.shape, F32) for s in segs],
        sem=("arbitrary",),
    )(*segs)
    return outs


def _lane_pick(x, h):
    lane = lax.broadcasted_iota(jnp.int32, x.shape, 1)
    return jnp.sum(jnp.where(lane == h, x, 0.0), axis=-1, keepdims=True)


def _fox_prompt_kernel(q_ref, k_ref, v_ref, cq_ref, ck_ref, o_in_ref, o_ref, *, nb):
    del o_in_ref
    hp = HEADS_PER_STEP
    g = pl.program_id(1)
    i = pl.program_id(2)
    tq = q_ref.shape[0]
    scale = HEAD_DIM ** -0.5

    def head(t, n_blocks):
        cols = slice(t * HEAD_DIM, (t + 1) * HEAD_DIM)
        q = q_ref[:, cols].astype(BF16)
        fq = _lane_pick(cq_ref[...], g * hp + t)
        m = jnp.full((tq, 1), NEG, F32)
        l = jnp.zeros((tq, 1), F32)
        acc = jnp.zeros((tq, HEAD_DIM), F32)
        for j in range(n_blocks):
            k = k_ref[j * tq:(j + 1) * tq, cols].astype(BF16)
            v = v_ref[j * tq:(j + 1) * tq, cols].astype(BF16)
            s = _dot_t(q, k) * scale + fq - ck_ref[t, j:j + 1, :]
            if j == n_blocks - 1:
                r = lax.broadcasted_iota(jnp.int32, s.shape, 0)
                c = lax.broadcasted_iota(jnp.int32, s.shape, 1)
                s = jnp.where(c <= r, s, NEG)
            m_new = jnp.maximum(m, jnp.max(s, axis=-1, keepdims=True))
            alpha = jnp.exp(m - m_new)
            p = jnp.exp(s - m_new)
            l = alpha * l + jnp.sum(p, axis=-1, keepdims=True)
            acc = alpha * acc + _dot(p.astype(BF16), v)
            m = m_new
        return acc * (1.0 / l)

    for n in range(nb):
        @pl.when(i == n)
        def _(n=n):
            outs = [head(t, n + 1) for t in range(hp)]
            o_ref[...] = jnp.concatenate(outs, axis=1).astype(o_ref.dtype)


def _fox_prompt(q, k, v, cum, cum_t, batch, seq, n_heads, m_total):
    hp = HEADS_PER_STEP
    assert n_heads % hp == 0
    nb = seq // Q_ROWS
    w = hp * HEAD_DIM
    o_init = jnp.zeros((m_total, n_heads * HEAD_DIM), BF16)
    return _pallas(
        functools.partial(_fox_prompt_kernel, nb=nb),
        grid=(batch, n_heads // hp, nb),
        in_specs=[
            pl.BlockSpec((Q_ROWS, w), lambda b, g, i: (b * nb + i, g)),
            pl.BlockSpec((seq, w), lambda b, g, i: (b, g)),
            pl.BlockSpec((seq, w), lambda b, g, i: (b, g)),
            pl.BlockSpec((Q_ROWS, LANES), lambda b, g, i: (b * nb + i, 0)),
            pl.BlockSpec((None, hp, nb, Q_ROWS), lambda b, g, i: (b, g, 0, 0)),
            pl.BlockSpec(memory_space=pl.ANY),
        ],
        out_specs=pl.BlockSpec((Q_ROWS, w), lambda b, g, i: (b * nb + i, g)),
        out_shape=jax.ShapeDtypeStruct(o_init.shape, o_init.dtype),
        input_output_aliases={5: 0},
        sem=("arbitrary", "arbitrary", "arbitrary"),
    )(q, k, v, cum, cum_t, o_init)


def _fox_sample_kernel(q_ref, k1_ref, v1_ref, k2_ref, v2_ref, cq_ref, ck1_ref, ck2_ref, o_in_ref, o_ref):
    del o_in_ref
    h = pl.program_id(1)
    scale = HEAD_DIM ** -0.5
    q = q_ref[...].astype(BF16)
    fq = _lane_pick(cq_ref[...], h)
    s1 = _dot_t(q, k1_ref[...].astype(BF16)) * scale + fq - ck1_ref[...]
    s2 = _dot_t(q, k2_ref[...].astype(BF16)) * scale + fq - ck2_ref[...]
    r = lax.broadcasted_iota(jnp.int32, s2.shape, 0)
    c = lax.broadcasted_iota(jnp.int32, s2.shape, 1)
    s2 = jnp.where(c <= r, s2, NEG)
    m = jnp.maximum(jnp.max(s1, axis=-1, keepdims=True), jnp.max(s2, axis=-1, keepdims=True))
    e1 = jnp.exp(s1 - m)
    e2 = jnp.exp(s2 - m)
    den = jnp.sum(e1, axis=-1, keepdims=True) + jnp.sum(e2, axis=-1, keepdims=True)
    out = _dot(e1.astype(BF16), v1_ref[...].astype(BF16)) + _dot(e2.astype(BF16), v2_ref[...].astype(BF16))
    o_ref[...] = (out * (1.0 / den)).astype(o_ref.dtype)


def _fox_sample(q, k, v, cache_k, cache_v, cum_new, cum_t_cache, cum_t_new, o, row0, t_new, n_heads):
    db, past, _ = cache_k.shape
    hd = HEAD_DIM
    rb0 = row0 // t_new
    new = pl.BlockSpec((t_new, hd), lambda b, h: (b, h))
    cache = pl.BlockSpec((None, past, hd), lambda b, h: (b, 0, h))
    return _pallas(
        _fox_sample_kernel,
        grid=(db, n_heads),
        in_specs=[
            new, cache, cache, new, new,
            pl.BlockSpec((t_new, LANES), lambda b, h: (b, 0)),
            pl.BlockSpec((None, None, 1, past), lambda b, h: (b, h, 0, 0)),
            pl.BlockSpec((None, None, 1, t_new), lambda b, h: (b, h, 0, 0)),
            pl.BlockSpec(memory_space=pl.ANY),
        ],
        out_specs=pl.BlockSpec((t_new, hd), lambda b, h: (rb0 + b, h)),
        out_shape=jax.ShapeDtypeStruct(o.shape, o.dtype),
        input_output_aliases={8: 0},
        sem=("arbitrary", "arbitrary"),
    )(q, cache_k, cache_v, k, v, cum_new, cum_t_cache, cum_t_new, o)


def _gather_rows_kernel(idx_ref, h_ref, o_ref, buf_ref, sem):
    rows = o_ref.shape[0]

    def copy(r):
        return pltpu.make_async_copy(h_ref.at[pl.ds(idx_ref[0, 0, r], 1), :], buf_ref.at[pl.ds(r, 1), :], sem)

    def start(r, c):
        copy(r).start()
        return c

    def wait(r, c):
        copy(r).wait()
        return c

    lax.fori_loop(0, rows, start, 0, unroll=8)
    lax.fori_loop(0, rows, wait, 0, unroll=8)
    o_ref[...] = buf_ref[...].astype(o_ref.dtype)


def _gather_rows(h, row_token, tile):
    p = row_token.shape[0]
    d = h.shape[1]
    nt = p // tile
    return _pallas(
        _gather_rows_kernel,
        grid=(nt,),
        in_specs=[
            pl.BlockSpec((1, 1, tile), lambda i: (i, 0, 0), memory_space=pltpu.SMEM),
            pl.BlockSpec(memory_space=pl.ANY),
        ],
        out_specs=pl.BlockSpec((tile, d), lambda i: (i, 0)),
        out_shape=jax.ShapeDtypeStruct((p, d), BF16),
        scratch_shapes=[pltpu.VMEM((tile, d), F32), pltpu.SemaphoreType.DMA(())],
        sem=("arbitrary",),
    )(row_token.reshape(nt, 1, tile), h)


def _moe_gateup_kernel(te_ref, nv_ref, a_ref, wg_ref, wu_ref, o_ref):
    del te_ref
    valid = pl.program_id(1) < nv_ref[0]

    @pl.when(valid)
    def _():
        a = a_ref[...]
        g = _dot(a, wg_ref[...].astype(BF16))
        u = _dot(a, wu_ref[...].astype(BF16))
        o_ref[...] = (_silu(g) * u).astype(o_ref.dtype)

    @pl.when(jnp.logical_not(valid))
    def _():
        o_ref[...] = jnp.zeros_like(o_ref)


def _moe_gateup(a, wg, wu, tile_expert, n_valid, tile):
    p, d = a.shape
    f = wg.shape[2]
    tn = _pick(f, (512, 256, 128))
    nt = p // tile

    def a_map(j, i, te, nv):
        return (jnp.minimum(i, nv[0] - 1), 0)

    def w_map(j, i, te, nv):
        return (te[jnp.minimum(i, nv[0] - 1)], 0, j)

    def o_map(j, i, te, nv):
        return (i, j)

    grid_spec = pltpu.PrefetchScalarGridSpec(
        num_scalar_prefetch=2,
        grid=(f // tn, nt),
        in_specs=[
            pl.BlockSpec((tile, d), a_map),
            pl.BlockSpec((None, d, tn), w_map),
            pl.BlockSpec((None, d, tn), w_map),
        ],
        out_specs=pl.BlockSpec((tile, tn), o_map),
    )
    return _pallas(
        _moe_gateup_kernel,
        grid_spec=grid_spec,
        out_shape=jax.ShapeDtypeStruct((p, f), BF16),
        sem=("arbitrary", "arbitrary"),
    )(tile_expert, n_valid, a, wg, wu)


def _moe_down_kernel(te_ref, nv_ref, a_ref, w_ref, o_ref, acc_ref, *, nk):
    del te_ref
    k = pl.program_id(2)
    valid = pl.program_id(1) < nv_ref[0]

    @pl.when(valid & (k == 0))
    def _():
        acc_ref[...] = _dot(a_ref[...], w_ref[...])

    @pl.when(valid & (k > 0))
    def _():
        acc_ref[...] += _dot(a_ref[...], w_ref[...])

    @pl.when(valid & (k == nk - 1))
    def _():
        o_ref[...] = acc_ref[...]

    @pl.when(jnp.logical_not(valid) & (k == nk - 1))
    def _():
        o_ref[...] = jnp.zeros_like(o_ref)


def _moe_down(a, w, tile_expert, n_valid, tile):
    p, f = a.shape
    d = w.shape[2]
    tn = _pick(d, (2048, 1024, 512, 256, 128))
    tk = _pick(f, (2048, 1024, 512, 256, 128))
    nk = f // tk
    nt = p // tile

    def clamp(i, k, nv):
        ok = i < nv[0]
        return jnp.where(ok, i, nv[0] - 1), jnp.where(ok, k, nk - 1)

    def a_map(j, i, k, te, nv):
        ie, ke = clamp(i, k, nv)
        return (ie, ke)

    def w_map(j, i, k, te, nv):
        ie, ke = clamp(i, k, nv)
        return (te[ie], ke, j)

    def o_map(j, i, k, te, nv):
        return (i, j)

    grid_spec = pltpu.PrefetchScalarGridSpec(
        num_scalar_prefetch=2,
        grid=(d // tn, nt, nk),
        in_specs=[
            pl.BlockSpec((tile, tk), a_map),
            pl.BlockSpec((None, tk, tn), w_map),
        ],
        out_specs=pl.BlockSpec((tile, tn), o_map),
        scratch_shapes=[pltpu.VMEM((tile, tn), F32)],
    )
    return _pallas(
        functools.partial(_moe_down_kernel, nk=nk),
        grid_spec=grid_spec,
        out_shape=jax.ShapeDtypeStruct((p, d), F32),
        sem=("arbitrary", "arbitrary", "arbitrary"),
    )(tile_expert, n_valid, a, w)


def _combine_kernel(dest_ref, y_ref, tg_ref, x_ref, gt_ref, o_ref, buf_ref, sem):
    tb = x_ref.shape[0]

    def copy(t, k):
        return pltpu.make_async_copy(y_ref.at[pl.ds(dest_ref[0, 0, TOP_K * t + k], 1), :],
                                     buf_ref.at[k, pl.ds(t, 1), :], sem)

    def start(t, c):
        for k in range(TOP_K):
            copy(t, k).start()
        return c

    def wait(t, c):
        for k in range(TOP_K):
            copy(t, k).wait()
        return c

    lax.fori_loop(0, tb, start, 0, unroll=8)
    lax.fori_loop(0, tb, wait, 0, unroll=8)
    tg = tg_ref[...]
    y = tg[:, 0:1] * buf_ref[0] + tg[:, 1:2] * buf_ref[1]
    o_ref[...] = x_ref[...] + _rows(gt_ref[...], tb) * y


def _combine(ys, dest, tg, x, gate, group):
    m, d = x.shape
    tb = _pick(m, (256,))
    nt = m // tb
    gpt = tb // group
    row = lambda i: (i, 0)
    return _pallas(
        _combine_kernel,
        grid=(nt,),
        in_specs=[
            pl.BlockSpec((1, 1, TOP_K * tb), lambda i: (i, 0, 0), memory_space=pltpu.SMEM),
            pl.BlockSpec(memory_space=pl.ANY),
            pl.BlockSpec((tb, LANES), row),
            pl.BlockSpec((tb, d), row),
            pl.BlockSpec((gpt, d), row),
        ],
        out_specs=pl.BlockSpec((tb, d), row),
        out_shape=jax.ShapeDtypeStruct((m, d), F32),
        scratch_shapes=[pltpu.VMEM((TOP_K, tb, d), F32), pltpu.SemaphoreType.DMA(())],
        sem=("arbitrary",),
    )(dest.reshape(nt, 1, TOP_K * tb), ys, tg, x, gate)


def _routing(ti, n_experts, tile):
    m = ti.shape[0]
    n_slots = m * TOP_K
    e_flat = ti[:, :TOP_K].reshape(n_slots)
    onehot = (e_flat[:, None] == jnp.arange(n_experts, dtype=jnp.int32)[None, :]).astype(jnp.int32)
    csum = jnp.cumsum(onehot, axis=0)
    rank = jnp.sum((csum - onehot) * onehot, axis=1)
    counts = csum[-1]
    tiles_e = (counts + tile - 1) // tile
    padded = tiles_e * tile
    start_pad = jnp.cumsum(padded) - padded
    start_raw = jnp.cumsum(counts) - counts
    dest = start_pad[e_flat] + rank
    nt = n_slots // tile + n_experts
    p = nt * tile
    tile_end = jnp.cumsum(tiles_e)
    n_valid = tile_end[-1:].astype(jnp.int32)
    tile_expert = jnp.minimum(
        jnp.sum((jnp.arange(nt, dtype=jnp.int32)[:, None] >= tile_end[None, :]).astype(jnp.int32), axis=1),
        n_experts - 1).astype(jnp.int32)
    slot_token = jnp.arange(n_slots, dtype=jnp.int32) // TOP_K
    _, sorted_token = lax.sort_key_val(e_flat, slot_token, is_stable=True)
    src = jnp.concatenate([sorted_token, jnp.zeros((n_slots,), jnp.int32)])
    row_token = jnp.zeros((p + n_slots,), jnp.int32)
    for e in range(n_experts):
        seg = lax.dynamic_slice(src, (start_raw[e],), (n_slots,))
        row_token = lax.dynamic_update_slice(row_token, seg, (start_pad[e],))
    return dest.astype(jnp.int32), row_token[:p], tile_expert, n_valid


def kernel(x_prompt, x_sample, cache_a_k, cache_a_v, cache_b_k, cache_b_v, cache_c_k, cache_c_v, cache_c_logf, c_prompt, c_sample, ada_w, ada_b, norm_g, w_in_ab, rel_bias_a, sinks_b, w_out_ab, ffn_w_gate, ffn_w_up, ffn_w_down, w_in_c, b_forget_c, w_out_c, router_w, router_b, moe_w_gate, moe_w_up, moe_w_down, final_norm_g):
    batch, seq, d = x_prompt.shape
    db, t_new, _ = x_sample.shape
    depth = ada_w.shape[0]
    ha = cache_a_k.shape[3]
    hkv_b = cache_b_k.shape[3]
    hb = sinks_b.shape[1]
    hc = cache_c_k.shape[3]
    past = cache_c_k.shape[2]
    la = cache_a_k.shape[2]
    lb = cache_b_k.shape[2]
    n_experts = router_w.shape[2]
    hd = HEAD_DIM
    tp = batch * seq
    ts = db * t_new
    m = tp + ts
    group = t_new
    assert seq % Q_ROWS == 0 and Q_ROWS % CHUNK == 0 and A_PREV_CHUNKS * CHUNK <= Q_ROWS
    assert group % 8 == 0 and seq % group == 0 and t_new <= CHUNK and past % CHUNK == 0
    assert hc <= LANES and n_experts <= LANES and tp % Q_ROWS == 0

    x = jnp.concatenate([x_prompt.reshape(tp, d), x_sample.reshape(ts, d)], axis=0)

    n_req = batch + db
    req_pad = -(-n_req // 8) * 8
    c_all = jnp.zeros((req_pad, d), F32).at[:n_req].set(jnp.concatenate([c_prompt, c_sample], axis=0))
    mod = _adaln(c_all, ada_w, ada_b)
    group_req = np.concatenate([np.repeat(np.arange(batch), seq // group), batch + np.arange(db)])

    def mod_params(layer):
        per_group = mod[layer][group_req]
        return [per_group[:, i * d:(i + 1) * d] for i in range(6)]

    states_even, states_odd = [], []
    for layer in range(depth):
        i = layer // 2
        sh_m, sc_m, gt_m, sh_f, sc_f, gt_f = mod_params(layer)
        h = _modulate(x, norm_g[layer, 0], sc_m, sh_m, group, BF16)
        if layer % 2 == 0:
            z = _mm(h, w_in_ab[i], 0, m, 0, w_in_ab.shape[2])
            ca, cb = ha, 3 * ha
            n_heads = ha + hb
            kvg = hb // hkv_b
            qcol = lambda hh: jnp.where(hh < ha, hh, cb + hh - ha)
            kcol = lambda hh: jnp.where(hh < ha, ca + hh, cb + hb + (hh - ha) // kvg)
            vcol = lambda hh: jnp.where(hh < ha, 2 * ca + hh, cb + hb + hkv_b + (hh - ha) // kvg)
            q_pos = Q_ROWS + np.arange(Q_ROWS)
            k_pos = np.arange(2 * Q_ROWS)
            ba, bb = _bias_tables(rel_bias_a[i], hb, q_pos, k_pos)
            sinks = jnp.concatenate([jnp.full((ha,), NEG, F32), sinks_b[i].astype(F32)])
            o = _band_prompt(z, jnp.concatenate([ba, bb], axis=0), sinks, batch, seq, m, (qcol, kcol, vcol))
            q_pos = past + np.arange(t_new)
            sa, _ = _bias_tables(rel_bias_a[i], hb, q_pos, past - la + np.arange(la + t_new))
            _, sb = _bias_tables(rel_bias_a[i], hb, q_pos, past - lb + np.arange(lb + t_new))
            o = _band_sample(z, cache_a_k[i].reshape(db, la, ha * hd), cache_a_v[i].reshape(db, la, ha * hd),
                             sa[:, :, :la], sa[:, :, la:], jnp.full((ha,), NEG, F32), o, tp, t_new,
                             (lambda hh: hh, lambda hh: ca + hh, lambda hh: 2 * ca + hh), 1, 0)
            o = _band_sample(z, cache_b_k[i].reshape(db, lb, hkv_b * hd), cache_b_v[i].reshape(db, lb, hkv_b * hd),
                             sb[:, :, :lb], sb[:, :, lb:], sinks_b[i].astype(F32), o, tp, t_new,
                             (lambda hh: cb + hh, lambda hh: cb + hb + hh // kvg, lambda hh: cb + hb + hkv_b + hh // kvg),
                             kvg, ha)
            x = _mm_resid(o, w_out_ab[i], x, gt_m, group)

            zp = z[:tp].reshape(batch, seq, -1)
            zs = z[tp:].reshape(db, t_new, -1)
            wa, wb = ha * hd, hb * hd
            wkv = hkv_b * hd
            ka_p, va_p = zp[:, :, wa:2 * wa], zp[:, :, 2 * wa:3 * wa]
            kb_p, vb_p = zp[:, :, 3 * wa + wb:3 * wa + wb + wkv], zp[:, :, 3 * wa + wb + wkv:]
            ka_s, va_s = zs[:, :, wa:2 * wa], zs[:, :, 2 * wa:3 * wa]
            kb_s, vb_s = zs[:, :, 3 * wa + wb:3 * wa + wb + wkv], zs[:, :, 3 * wa + wb + wkv:]
            lap, lbp = min(la, seq), min(lb, seq)
            heads = lambda t, nh: t.reshape(t.shape[0], t.shape[1], nh, hd)
            roll = lambda cache, new, nh: jnp.concatenate([cache, heads(new, nh)], axis=1)[:, -cache.shape[1]:]
            states_even.append((
                heads(ka_p[:, -lap:], ha), heads(va_p[:, -lap:], ha),
                roll(cache_a_k[i], ka_s, ha), roll(cache_a_v[i], va_s, ha),
                heads(kb_p[:, -lbp:], hkv_b), heads(vb_p[:, -lbp:], hkv_b),
                roll(cache_b_k[i], kb_s, hkv_b), roll(cache_b_v[i], vb_s, hkv_b)))

            h = _modulate(x, norm_g[layer, 1], sc_f, sh_f, group, BF16)
            f = ffn_w_gate.shape[2]
            fpad = -(-f // 1024) * 1024 if f > 1024 else f
            padc = lambda w: jnp.pad(w.astype(BF16), ((0, 0), (0, fpad - f)))
            mid = _mm_swiglu(h, padc(ffn_w_gate[i]), padc(ffn_w_up[i]))
            wd = jnp.pad(ffn_w_down[i].astype(BF16), ((0, fpad - f), (0, 0)))
            x = _mm_resid(mid, wd, x, gt_f, group)
        else:
            wc = hc * hd
            q_p, k_p, v_p = (_mm(h, w_in_c[i], 0, tp, c * wc, wc) for c in range(3))
            q_s, k_s, v_s = (_mm(h, w_in_c[i], tp, ts, c * wc, wc) for c in range(3))
            wgate = jnp.zeros((d, LANES), BF16).at[:, :hc].set(w_in_c[i][:, 3 * wc:].astype(BF16))
            bgate = jnp.zeros((1, LANES), F32).at[0, :hc].set(b_forget_c[i].astype(F32))
            logf = _mm_logsig(h, wgate, bgate)
            logf_p = logf[:tp].reshape(batch, seq, LANES)
            logf_s = logf[tp:].reshape(db, t_new, LANES)
            cum_p, = _cumsum([logf_p])
            cache_lf = jnp.pad(cache_c_logf[i].astype(F32), ((0, 0), (0, 0), (0, LANES - hc)))
            cum_c, cum_s = _cumsum([cache_lf, logf_s])
            nb = seq // Q_ROWS
            cum_t_p = jnp.swapaxes(cum_p[:, :, :hc], 1, 2).reshape(batch, hc, nb, Q_ROWS)
            o = _fox_prompt(q_p, k_p, v_p, cum_p.reshape(tp, LANES), cum_t_p, batch, seq, hc, m)
            cum_t_c = jnp.swapaxes(cum_c[:, :, :hc], 1, 2).reshape(db, hc, 1, past)
            cum_t_s = jnp.swapaxes(cum_s[:, :, :hc], 1, 2).reshape(db, hc, 1, t_new)
            o = _fox_sample(q_s, k_s, v_s, cache_c_k[i].reshape(db, past, wc), cache_c_v[i].reshape(db, past, wc),
                            cum_s.reshape(ts, LANES), cum_t_c, cum_t_s, o, tp, t_new, hc)
            x = _mm_resid(o, w_out_c[i], x, gt_m, group)

            states_odd.append((k_p.reshape(batch, seq, hc, hd), v_p.reshape(batch, seq, hc, hd), logf_p[:, :, :hc],
                               k_s.reshape(db, t_new, hc, hd), v_s.reshape(db, t_new, hc, hd), logf_s[:, :, :hc]))

            h32, ti, tg = _route(x, norm_g[layer, 1], sc_f, sh_f, router_w[i], router_b[i], group)
            tile = _pick(m * TOP_K, (512, 256))
            dest, row_token, tile_expert, n_valid = _routing(ti, n_experts, tile)
            hs = _gather_rows(h32, row_token, tile)
            mid = _moe_gateup(hs, moe_w_gate[i], moe_w_up[i], tile_expert, n_valid, tile)
            ys = _moe_down(mid, moe_w_down[i].astype(BF16), tile_expert, n_valid, tile)
            x = _combine(ys, dest, tg, x, gt_f, group)

    y_prompt = _final_norm(x, final_norm_g, 0, tp).reshape(batch, seq, d)
    y_sample = _final_norm(x, final_norm_g, tp, ts).reshape(db, t_new, d)
    even = [jnp.stack(s) for s in zip(*states_even)]
    odd = [jnp.stack(s) for s in zip(*states_odd)]
    return (y_prompt, y_sample, *even, *odd)
```

```python
import functools

import numpy as np
import jax
import jax.numpy as jnp
from jax import lax
from jax.experimental import pallas as pl
from jax.experimental.pallas import tpu as pltpu

F32 = jnp.float32
BF16 = jnp.bfloat16

HEAD_DIM = 128
CHUNK = 64
A_PREV_CHUNKS = 8
B_PREV_CHUNKS = 2
TOP_K = 2
EPS = 1e-6
NEG = -1e30
LANES = 128
VMEM_LIMIT = 56 * 1024 * 1024
Q_ROWS = 512


def _pick(n, prefs):
    for p in prefs:
        if p <= n and n % p == 0:
            return p
    return n


def _pallas(body, sem, **kw):
    fn = body.func if isinstance(body, functools.partial) else body
    params = pltpu.CompilerParams(dimension_semantics=sem, vmem_limit_bytes=VMEM_LIMIT)
    return pl.pallas_call(body, name=fn.__name__.strip("_"), compiler_params=params, **kw)


def _rows(p, tm):
    g, n = p.shape
    if g == tm:
        return p
    return jnp.broadcast_to(p[:, None, :], (g, tm // g, n)).reshape(tm, n)


def _dot(a, b):
    return jnp.dot(a, b, preferred_element_type=F32)


def _dot_t(a, b):
    return lax.dot_general(a, b, (((1,), (1,)), ((), ())), preferred_element_type=F32)


def _silu(x):
    return x / (1.0 + jnp.exp(-x))


def _adaln_kernel(c_ref, w_ref, b_ref, o_ref):
    a = _silu(c_ref[...]).astype(BF16)
    o_ref[...] = _dot(a, w_ref[...].astype(BF16)) + b_ref[...]


def _adaln(c, ada_w, ada_b):
    depth, d, n = ada_w.shape
    rows = c.shape[0]
    tn = _pick(n, (512, 256, 128))
    return _pallas(
        _adaln_kernel,
        grid=(depth, n // tn),
        in_specs=[
            pl.BlockSpec((rows, d), lambda l, j: (0, 0)),
            pl.BlockSpec((None, d, tn), lambda l, j: (l, 0, j)),
            pl.BlockSpec((None, 1, tn), lambda l, j: (l, 0, j)),
        ],
        out_specs=pl.BlockSpec((None, rows, tn), lambda l, j: (l, 0, j)),
        out_shape=jax.ShapeDtypeStruct((depth, rows, n), F32),
        sem=("arbitrary", "arbitrary"),
    )(c, ada_w, ada_b.reshape(depth, 1, n))


def _normed(x, g):
    ms = jnp.mean(x * x, axis=-1, keepdims=True)
    return x * lax.rsqrt(ms + EPS) * g


def _modulate_kernel(x_ref, g_ref, sc_ref, sh_ref, o_ref):
    tm = x_ref.shape[0]
    y = _normed(x_ref[...], g_ref[...])
    y = y * (1.0 + _rows(sc_ref[...], tm)) + _rows(sh_ref[...], tm)
    o_ref[...] = y.astype(o_ref.dtype)


def _modulate(x, g, sc, sh, group, out_dtype):
    m, d = x.shape
    tm = _pick(m, (256,))
    gpt = tm // group
    return _pallas(
        _modulate_kernel,
        grid=(m // tm,),
        in_specs=[
            pl.BlockSpec((tm, d), lambda i: (i, 0)),
            pl.BlockSpec((1, d), lambda i: (0, 0)),
            pl.BlockSpec((gpt, d), lambda i: (i, 0)),
            pl.BlockSpec((gpt, d), lambda i: (i, 0)),
        ],
        out_specs=pl.BlockSpec((tm, d), lambda i: (i, 0)),
        out_shape=jax.ShapeDtypeStruct((m, d), out_dtype),
        sem=("arbitrary",),
    )(x, g.reshape(1, d), sc, sh)


def _final_norm_kernel(x_ref, g_ref, o_ref):
    o_ref[...] = _normed(x_ref[...], g_ref[...])


def _final_norm(x, g, row0, rows):
    m, d = x.shape
    tm = _pick(rows, (256,))
    assert row0 % tm == 0
    off = row0 // tm
    return _pallas(
        _final_norm_kernel,
        grid=(rows // tm,),
        in_specs=[
            pl.BlockSpec((tm, d), lambda i: (i + off, 0)),
            pl.BlockSpec((1, d), lambda i: (0, 0)),
        ],
        out_specs=pl.BlockSpec((tm, d), lambda i: (i, 0)),
        out_shape=jax.ShapeDtypeStruct((rows, d), F32),
        sem=("arbitrary",),
    )(x, g.reshape(1, d))


def _route_kernel(x_ref, g_ref, sc_ref, sh_ref, rw_ref, rb_ref, h_ref, ti_ref, tg_ref, *, n_experts):
    tm = x_ref.shape[0]
    y = _normed(x_ref[...], g_ref[...])
    y = y * (1.0 + _rows(sc_ref[...], tm)) + _rows(sh_ref[...], tm)
    h_ref[...] = y
    logits = _dot(y.astype(BF16), rw_ref[...]) + rb_ref[...]
    lane = lax.broadcasted_iota(jnp.int32, logits.shape, 1)
    l1 = jnp.where(lane < n_experts, logits, NEG)
    m1 = jnp.max(l1, axis=-1, keepdims=True)
    i1 = jnp.min(jnp.where(l1 == m1, lane, LANES), axis=-1, keepdims=True)
    l2 = jnp.where(lane == i1, NEG, l1)
    m2 = jnp.max(l2, axis=-1, keepdims=True)
    i2 = jnp.min(jnp.where(l2 == m2, lane, LANES), axis=-1, keepdims=True)
    e2 = jnp.exp(m2 - m1)
    den = 1.0 + e2
    ti_ref[...] = jnp.where(lane == 0, i1, jnp.where(lane == 1, i2, 0))
    tg_ref[...] = jnp.where(lane == 0, 1.0 / den, jnp.where(lane == 1, e2 / den, 0.0))


def _route(x, g, sc, sh, rw, rb, group):
    m, d = x.shape
    n_experts = rw.shape[1]
    tm = _pick(m, (256,))
    gpt = tm // group
    rw_p = jnp.zeros((d, LANES), BF16).at[:, :n_experts].set(rw.astype(BF16))
    rb_p = jnp.zeros((1, LANES), F32).at[0, :n_experts].set(rb.astype(F32))
    row = lambda i: (i, 0)
    fixed = lambda i: (0, 0)
    return _pallas(
        functools.partial(_route_kernel, n_experts=n_experts),
        grid=(m // tm,),
        in_specs=[
            pl.BlockSpec((tm, d), row),
            pl.BlockSpec((1, d), fixed),
            pl.BlockSpec((gpt, d), row),
            pl.BlockSpec((gpt, d), row),
            pl.BlockSpec((d, LANES), fixed),
            pl.BlockSpec((1, LANES), fixed),
        ],
        out_specs=[
            pl.BlockSpec((tm, d), row),
            pl.BlockSpec((tm, LANES), row),
            pl.BlockSpec((tm, LANES), row),
        ],
        out_shape=[
            jax.ShapeDtypeStruct((m, d), F32),
            jax.ShapeDtypeStruct((m, LANES), jnp.int32),
            jax.ShapeDtypeStruct((m, LANES), F32),
        ],
        sem=("arbitrary",),
    )(x, g.reshape(1, d), sc, sh, rw_p, rb_p)


def _mm_kernel(a_ref, w_ref, o_ref):
    o_ref[...] = _dot(a_ref[...], w_ref[...].astype(BF16))


def _mm(a, w, layer, row0, rows, col0, cols):
    k = a.shape[1]
    tm = _pick(rows, (512, 256))
    tn = _pick(cols, (1024, 512, 256, 128))
    assert row0 % tm == 0 and col0 % tn == 0
    ri, cj = row0 // tm, col0 // tn
    return _pallas(
        _mm_kernel,
        grid=(cols // tn, rows // tm),
        in_specs=[
            pl.BlockSpec((tm, k), lambda j, i: (i + ri, 0)),
            pl.BlockSpec((None, k, tn), lambda j, i: (layer, 0, j + cj)),
        ],
        out_specs=pl.BlockSpec((tm, tn), lambda j, i: (i, j)),
        out_shape=jax.ShapeDtypeStruct((rows, cols), F32),
        sem=("arbitrary", "arbitrary"),
    )(a, w)


def _mm_logsig_kernel(a_ref, w_ref, b_ref, o_ref):
    z = _dot(a_ref[...], w_ref[...]) + b_ref[...]
    o_ref[...] = jnp.minimum(z, 0.0) - jnp.log1p(jnp.exp(-jnp.abs(z)))


def _mm_logsig(a, w, b):
    m, k = a.shape
    n = w.shape[1]
    tm = _pick(m, (512, 256))
    return _pallas(
        _mm_logsig_kernel,
        grid=(m // tm,),
        in_specs=[
            pl.BlockSpec((tm, k), lambda i: (i, 0)),
            pl.BlockSpec((k, n), lambda i: (0, 0)),
            pl.BlockSpec((1, n), lambda i: (0, 0)),
        ],
        out_specs=pl.BlockSpec((tm, n), lambda i: (i, 0)),
        out_shape=jax.ShapeDtypeStruct((m, n), F32),
        sem=("arbitrary",),
    )(a, w, b)


def _mm_swiglu_kernel(a_ref, wg_ref, wu_ref, o_ref):
    a = a_ref[...]
    g = _dot(a, wg_ref[...].astype(BF16))
    u = _dot(a, wu_ref[...].astype(BF16))
    o_ref[...] = (_silu(g) * u).astype(o_ref.dtype)


def _mm_swiglu(a, wg, wu):
    m, k = a.shape
    n = wg.shape[1]
    tm = _pick(m, (512, 256))
    tn = _pick(n, (512, 256, 128))
    return _pallas(
        _mm_swiglu_kernel,
        grid=(n // tn, m // tm),
        in_specs=[
            pl.BlockSpec((tm, k), lambda j, i: (i, 0)),
            pl.BlockSpec((k, tn), lambda j, i: (0, j)),
            pl.BlockSpec((k, tn), lambda j, i: (0, j)),
        ],
        out_specs=pl.BlockSpec((tm, tn), lambda j, i: (i, j)),
        out_shape=jax.ShapeDtypeStruct((m, n), BF16),
        sem=("arbitrary", "arbitrary"),
    )(a, wg, wu)


def _mm_resid_kernel(a_ref, w_ref, x_ref, g_ref, o_ref, *scratch, nk):
    tm = a_ref.shape[0]
    part = _dot(a_ref[...], w_ref[...].astype(BF16))
    if nk == 1:
        o_ref[...] = x_ref[...] + _rows(g_ref[...], tm) * part
        return
    acc_ref, = scratch
    k = pl.program_id(2)

    @pl.when(k == 0)
    def _():
        acc_ref[...] = part

    @pl.when(k > 0)
    def _():
        acc_ref[...] += part

    @pl.when(k == nk - 1)
    def _():
        o_ref[...] = x_ref[...] + _rows(g_ref[...], tm) * acc_ref[...]


def _mm_resid(a, w, layer, x, gate, group):
    m, kdim = a.shape
    n = w.shape[2]
    tk = kdim if kdim <= 4096 else _pick(kdim, (1024, 512, 256, 128))
    nk = kdim // tk
    tm = _pick(m, (1536, 512, 256)) if nk > 1 else _pick(m, (512, 256))
    tn = _pick(n, (1024, 512, 256, 128))
    gpt = tm // group
    scratch = [] if nk == 1 else [pltpu.VMEM((tm, tn), F32)]
    return _pallas(
        functools.partial(_mm_resid_kernel, nk=nk),
        grid=(n // tn, m // tm, nk),
        in_specs=[
            pl.BlockSpec((tm, tk), lambda j, i, k: (i, k)),
            pl.BlockSpec((None, tk, tn), lambda j, i, k: (layer, k, j)),
            pl.BlockSpec((tm, tn), lambda j, i, k: (i, j)),
            pl.BlockSpec((gpt, tn), lambda j, i, k: (i, j)),
        ],
        out_specs=pl.BlockSpec((tm, tn), lambda j, i, k: (i, j)),
        out_shape=jax.ShapeDtypeStruct((m, n), F32),
        scratch_shapes=scratch,
        sem=("arbitrary", "arbitrary", "arbitrary"),
    )(a, w, x, gate)


def _band_core(q, k1, v1, k2, v2, b1, b2, sink, first_ok):
    scale = HEAD_DIM ** -0.5
    qb = q.astype(BF16)
    s1 = _dot_t(qb, k1.astype(BF16)) * scale + b1
    s2 = _dot_t(qb, k2.astype(BF16)) * scale + b2
    if first_ok is not None:
        s1 = jnp.where(first_ok, s1, NEG)
    m = jnp.maximum(jnp.max(s1, axis=-1, keepdims=True), jnp.max(s2, axis=-1, keepdims=True))
    m = jnp.maximum(m, sink)
    e1 = jnp.exp(s1 - m)
    e2 = jnp.exp(s2 - m)
    den = jnp.sum(e1, axis=-1, keepdims=True) + jnp.sum(e2, axis=-1, keepdims=True) + jnp.exp(sink - m)
    out = _dot(e1.astype(BF16), v1.astype(BF16)) + _dot(e2.astype(BF16), v2.astype(BF16))
    return out * (1.0 / den)


HEADS_PER_STEP = 2


def _band_prompt_kernel(sink_ref, q_ref, *refs):
    hp = HEADS_PER_STEP
    kv = refs[:4 * hp]
    bp_ref, bc_ref, _, o_ref = refs[4 * hp:]
    g = pl.program_id(0)
    blk = pl.program_id(2)
    outs = []
    for t in range(hp):
        kp_ref, kc_ref, vp_ref, vc_ref = kv[4 * t:4 * t + 4]
        q = q_ref[:, t * HEAD_DIM:(t + 1) * HEAD_DIM]
        outs.append(_band_core(q, kp_ref[...], vp_ref[...], kc_ref[...], vc_ref[...],
                               bp_ref[t], bc_ref[t], sink_ref[g * hp + t], blk > 0))
    o_ref[...] = jnp.concatenate(outs, axis=1).astype(o_ref.dtype)


def _band_prompt(z, bias, sinks, batch, seq, m_total, cols):
    qcol, kcol, vcol = cols
    hp = HEADS_PER_STEP
    n_heads = bias.shape[0]
    assert n_heads % hp == 0
    nb = seq // Q_ROWS
    hd = HEAD_DIM

    def rows(b, i):
        return b * nb + i

    def prev_rows(b, i):
        return b * nb + jnp.maximum(i - 1, 0)

    def spec(row_fn, col_fn, t):
        return pl.BlockSpec((Q_ROWS, hd), lambda g, b, i, s: (row_fn(b, i), col_fn(g * hp + t)))

    kv_specs = []
    for t in range(hp):
        kv_specs += [spec(prev_rows, kcol, t), spec(rows, kcol, t), spec(prev_rows, vcol, t), spec(rows, vcol, t)]
    grid_spec = pltpu.PrefetchScalarGridSpec(
        num_scalar_prefetch=1,
        grid=(n_heads // hp, batch, nb),
        in_specs=[
            pl.BlockSpec((Q_ROWS, hp * hd), lambda g, b, i, s: (rows(b, i), qcol(g * hp) // hp)),
            *kv_specs,
            pl.BlockSpec((hp, Q_ROWS, Q_ROWS), lambda g, b, i, s: (g, 0, 0)),
            pl.BlockSpec((hp, Q_ROWS, Q_ROWS), lambda g, b, i, s: (g, 0, 1)),
            pl.BlockSpec(memory_space=pl.ANY),
        ],
        out_specs=pl.BlockSpec((Q_ROWS, hp * hd), lambda g, b, i, s: (rows(b, i), g)),
    )
    o_init = jnp.zeros((m_total, n_heads * hd), BF16)
    n_in = 1 + 1 + 4 * hp + 2
    return _pallas(
        _band_prompt_kernel,
        grid_spec=grid_spec,
        out_shape=jax.ShapeDtypeStruct(o_init.shape, o_init.dtype),
        input_output_aliases={n_in: 0},
        sem=("arbitrary", "arbitrary", "arbitrary"),
    )(sinks, z, *([z] * (4 * hp)), bias, bias, o_init)


def _band_sample_kernel(sink_ref, q_ref, k1_ref, v1_ref, k2_ref, v2_ref, b1_ref, b2_ref, o_in_ref, o_ref):
    del o_in_ref
    h = pl.program_id(0)
    out = _band_core(q_ref[...], k1_ref[...], v1_ref[...], k2_ref[...], v2_ref[...],
                     b1_ref[...], b2_ref[...], sink_ref[h], None)
    o_ref[...] = out.astype(o_ref.dtype)


def _band_sample(z, cache_k, cache_v, bias1, bias2, sinks, o, row0, t_new, cols, kv_group, out_col0):
    qcol, kcol, vcol = cols
    n_heads = bias1.shape[0]
    db, n_cache, _ = cache_k.shape
    hd = HEAD_DIM
    rb0 = row0 // t_new

    new = lambda col_fn: pl.BlockSpec((t_new, hd), lambda h, b, s: (rb0 + b, col_fn(h)))
    cache = pl.BlockSpec((None, n_cache, hd), lambda h, b, s: (b, 0, h // kv_group))
    grid_spec = pltpu.PrefetchScalarGridSpec(
        num_scalar_prefetch=1,
        grid=(n_heads, db),
        in_specs=[
            new(qcol), cache, cache, new(kcol), new(vcol),
            pl.BlockSpec((None, t_new, n_cache), lambda h, b, s: (h, 0, 0)),
            pl.BlockSpec((None, t_new, t_new), lambda h, b, s: (h, 0, 0)),
            pl.BlockSpec(memory_space=pl.ANY),
        ],
        out_specs=pl.BlockSpec((t_new, hd), lambda h, b, s: (rb0 + b, out_col0 + h)),
    )
    return _pallas(
        _band_sample_kernel,
        grid_spec=grid_spec,
        out_shape=jax.ShapeDtypeStruct(o.shape, o.dtype),
        input_output_aliases={8: 0},
        sem=("arbitrary", "arbitrary"),
    )(sinks, z, cache_k, cache_v, z, z, bias1, bias2, o)


def _chunk_band_mask(q_pos, k_pos, n_prev):
    qc = q_pos[:, None] // CHUNK
    kc = k_pos[None, :] // CHUNK
    return (kc <= qc) & (kc >= qc - n_prev) & (k_pos[None, :] >= 0)


def _rel_bias(table, q_pos, k_pos):
    clip = (table.shape[0] - 1) // 2
    nq, nk = len(q_pos), len(k_pos)
    diff = (q_pos[-1] - k_pos[0]) - np.arange(nq + nk - 1)
    vec = table[np.clip(diff, -clip, clip) + clip].astype(F32).T
    n_heads, length = vec.shape
    padded = jnp.concatenate([vec, jnp.zeros((n_heads, 1), F32)], axis=1)
    g = jnp.tile(padded, (1, nq))[:, :nq * length].reshape(n_heads, nq, length)
    return g[:, :, nq - 1:nq - 1 + nk]


def _alibi_bias(n_heads, q_pos, k_pos):
    slopes = 2.0 ** (-8.0 * jnp.arange(1, n_heads + 1, dtype=F32) / n_heads)
    dist = jnp.asarray(np.abs(q_pos[:, None] - k_pos[None, :]), F32)
    return -slopes[:, None, None] * dist


def _bias_tables(rel_table, n_b_heads, q_pos, k_pos):
    ma = _chunk_band_mask(q_pos, k_pos, A_PREV_CHUNKS)
    mb = _chunk_band_mask(q_pos, k_pos, B_PREV_CHUNKS)
    ba = jnp.where(ma[None], _rel_bias(rel_table, q_pos, k_pos), NEG)
    bb = jnp.where(mb[None], _alibi_bias(n_b_heads, q_pos, k_pos), NEG)
    return ba, bb


def _cumsum_kernel(*refs, seg_lens, blk):
    n = len(seg_lens)
    in_refs, out_refs = refs[:n], refs[n:]
    carry = jnp.zeros((1, in_refs[0].shape[-1]), F32)
    for x_ref, o_ref, length in zip(in_refs, out_refs, seg_lens):
        for start in range(0, length, blk):
            size = min(blk, length - start)
            r = lax.broadcasted_iota(jnp.int32, (size, size), 0)
            c = lax.broadcasted_iota(jnp.int32, (size, size), 1)
            tri = jnp.where(c <= r, 1.0, 0.0).astype(F32)
            x = x_ref[start:start + size, :]
            cs = jnp.dot(tri, x, preferred_element_type=F32, precision=lax.Precision.HIGHEST) + carry
            o_ref[start:start + size, :] = cs
            carry = cs[size - 1:size, :]


def _cumsum(segs):
    g = segs[0].shape[0]
    seg_lens = tuple(s.shape[1] for s in segs)
    specs = [pl.BlockSpec((None, t, LANES), lambda i: (i, 0, 0)) for t in seg_lens]
    outs = _pallas(
        functools.partial(_cumsum_kernel, seg_lens=seg_lens, blk=256),
        grid=(g,),
        in_specs=specs,
        out_specs=specs,
        out_shape=[jax.ShapeDtypeStruct(s.shape, F32) for s in segs],
        sem=("arbitrary",),
    )(*segs)
    return outs


def _lane_pick(x, h):
    lane = lax.broadcasted_iota(jnp.int32, x.shape, 1)
    return jnp.sum(jnp.where(lane == h, x, 0.0), axis=-1, keepdims=True)


def _fox_prompt_kernel(q_ref, k_ref, v_ref, cq_ref, ck_ref, o_in_ref, o_ref, *, nb):
    del o_in_ref
    hp = HEADS_PER_STEP
    g = pl.program_id(1)
    i = pl.program_id(2)
    tq = q_ref.shape[0]
    scale = HEAD_DIM ** -0.5

    def head(t, n_blocks):
        cols = slice(t * HEAD_DIM, (t + 1) * HEAD_DIM)
        q = q_ref[:, cols].astype(BF16)
        fq = _lane_pick(cq_ref[...], g * hp + t)
        m = jnp.full((tq, 1), NEG, F32)
        l = jnp.zeros((tq, 1), F32)
        acc = jnp.zeros((tq, HEAD_DIM), F32)
        for j in range(n_blocks):
            k = k_ref[j * tq:(j + 1) * tq, cols].astype(BF16)
            v = v_ref[j * tq:(j + 1) * tq, cols].astype(BF16)
            s = _dot_t(q, k) * scale + fq - ck_ref[t, j:j + 1, :]
            if j == n_blocks - 1:
                r = lax.broadcasted_iota(jnp.int32, s.shape, 0)
                c = lax.broadcasted_iota(jnp.int32, s.shape, 1)
                s = jnp.where(c <= r, s, NEG)
            m_new = jnp.maximum(m, jnp.max(s, axis=-1, keepdims=True))
            alpha = jnp.exp(m - m_new)
            p = jnp.exp(s - m_new)
            l = alpha * l + jnp.sum(p, axis=-1, keepdims=True)
            acc = alpha * acc + _dot(p.astype(BF16), v)
            m = m_new
        return acc * (1.0 / l)

    for n in range(nb):
        @pl.when(i == n)
        def _(n=n):
            outs = [head(t, n + 1) for t in range(hp)]
            o_ref[...] = jnp.concatenate(outs, axis=1).astype(o_ref.dtype)


def _fox_prompt(q, k, v, cum, cum_t, batch, seq, n_heads, m_total):
    hp = HEADS_PER_STEP
    assert n_heads % hp == 0
    nb = seq // Q_ROWS
    w = hp * HEAD_DIM
    o_init = jnp.zeros((m_total, n_heads * HEAD_DIM), BF16)
    return _pallas(
        functools.partial(_fox_prompt_kernel, nb=nb),
        grid=(batch, n_heads // hp, nb),
        in_specs=[
            pl.BlockSpec((Q_ROWS, w), lambda b, g, i: (b * nb + i, g)),
            pl.BlockSpec((seq, w), lambda b, g, i: (b, g)),
            pl.BlockSpec((seq, w), lambda b, g, i: (b, g)),
            pl.BlockSpec((Q_ROWS, LANES), lambda b, g, i: (b * nb + i, 0)),
            pl.BlockSpec((None, hp, nb, Q_ROWS), lambda b, g, i: (b, g, 0, 0)),
            pl.BlockSpec(memory_space=pl.ANY),
        ],
        out_specs=pl.BlockSpec((Q_ROWS, w), lambda b, g, i: (b * nb + i, g)),
        out_shape=jax.ShapeDtypeStruct(o_init.shape, o_init.dtype),
        input_output_aliases={5: 0},
        sem=("arbitrary", "arbitrary", "arbitrary"),
    )(q, k, v, cum, cum_t, o_init)


def _fox_sample_kernel(q_ref, kc_ref, vc_ref, k2_ref, v2_ref, cq_ref, ck1_ref, ck2_ref, o_in_ref, o_ref,
                       kbuf_ref, vbuf_ref, sem):
    del o_in_ref
    b = pl.program_id(0)
    h = pl.program_id(1)
    n_h = pl.num_programs(1)
    step = b * n_h + h
    n_steps = pl.num_programs(0) * n_h
    slot = step % 2

    def copies(bb, hh, sl):
        return (pltpu.make_async_copy(kc_ref.at[bb, :, hh, :], kbuf_ref.at[sl], sem.at[0, sl]),
                pltpu.make_async_copy(vc_ref.at[bb, :, hh, :], vbuf_ref.at[sl], sem.at[1, sl]))

    @pl.when(step == 0)
    def _():
        for c in copies(b, h, slot):
            c.start()

    @pl.when(step + 1 < n_steps)
    def _():
        nxt = step + 1
        for c in copies(nxt // n_h, nxt % n_h, 1 - slot):
            c.start()

    for c in copies(b, h, slot):
        c.wait()
    k1_ref = kbuf_ref.at[slot]
    v1_ref = vbuf_ref.at[slot]
    scale = HEAD_DIM ** -0.5
    q = q_ref[...].astype(BF16)
    fq = _lane_pick(cq_ref[...], h)
    s1 = _dot_t(q, k1_ref[...].astype(BF16)) * scale + fq - ck1_ref[...]
    s2 = _dot_t(q, k2_ref[...].astype(BF16)) * scale + fq - ck2_ref[...]
    r = lax.broadcasted_iota(jnp.int32, s2.shape, 0)
    c = lax.broadcasted_iota(jnp.int32, s2.shape, 1)
    s2 = jnp.where(c <= r, s2, NEG)
    m = jnp.maximum(jnp.max(s1, axis=-1, keepdims=True), jnp.max(s2, axis=-1, keepdims=True))
    e1 = jnp.exp(s1 - m)
    e2 = jnp.exp(s2 - m)
    den = jnp.sum(e1, axis=-1, keepdims=True) + jnp.sum(e2, axis=-1, keepdims=True)
    out = _dot(e1.astype(BF16), v1_ref[...].astype(BF16)) + _dot(e2.astype(BF16), v2_ref[...].astype(BF16))
    o_ref[...] = (out * (1.0 / den)).astype(o_ref.dtype)


def _fox_sample(q, k, v, cache_k, cache_v, cum_new, cum_t_cache, cum_t_new, o, row0, t_new, n_heads):
    db, past = cache_k.shape[:2]
    hd = HEAD_DIM
    rb0 = row0 // t_new
    new = pl.BlockSpec((t_new, hd), lambda b, h: (b, h))
    cache = pl.BlockSpec(memory_space=pl.ANY)
    return _pallas(
        _fox_sample_kernel,
        grid=(db, n_heads),
        in_specs=[
            new, cache, cache, new, new,
            pl.BlockSpec((t_new, LANES), lambda b, h: (b, 0)),
            pl.BlockSpec((None, None, 1, past), lambda b, h: (b, h, 0, 0)),
            pl.BlockSpec((None, None, 1, t_new), lambda b, h: (b, h, 0, 0)),
            pl.BlockSpec(memory_space=pl.ANY),
        ],
        out_specs=pl.BlockSpec((t_new, hd), lambda b, h: (rb0 + b, h)),
        out_shape=jax.ShapeDtypeStruct(o.shape, o.dtype),
        scratch_shapes=[pltpu.VMEM((2, past, hd), cache_k.dtype), pltpu.VMEM((2, past, hd), cache_v.dtype),
                        pltpu.SemaphoreType.DMA((2, 2))],
        input_output_aliases={8: 0},
        sem=("arbitrary", "arbitrary"),
    )(q, cache_k, cache_v, k, v, cum_new, cum_t_cache, cum_t_new, o)


def _gather_rows_kernel(idx_ref, h_ref, o_ref, buf_ref, sem):
    rows = o_ref.shape[0]

    def copy(r):
        return pltpu.make_async_copy(h_ref.at[pl.ds(idx_ref[0, 0, r], 1), :], buf_ref.at[pl.ds(r, 1), :], sem)

    def start(r, c):
        copy(r).start()
        return c

    def wait(r, c):
        copy(r).wait()
        return c

    lax.fori_loop(0, rows, start, 0, unroll=8)
    lax.fori_loop(0, rows, wait, 0, unroll=8)
    o_ref[...] = buf_ref[...].astype(o_ref.dtype)


def _gather_rows(h, row_token, tile):
    p = row_token.shape[0]
    d = h.shape[1]
    nt = p // tile
    return _pallas(
        _gather_rows_kernel,
        grid=(nt,),
        in_specs=[
            pl.BlockSpec((1, 1, tile), lambda i: (i, 0, 0), memory_space=pltpu.SMEM),
            pl.BlockSpec(memory_space=pl.ANY),
        ],
        out_specs=pl.BlockSpec((tile, d), lambda i: (i, 0)),
        out_shape=jax.ShapeDtypeStruct((p, d), BF16),
        scratch_shapes=[pltpu.VMEM((tile, d), F32), pltpu.SemaphoreType.DMA(())],
        sem=("arbitrary",),
    )(row_token.reshape(nt, 1, tile), h)


def _moe_gateup_kernel(te_ref, nv_ref, a_ref, wg_ref, wu_ref, o_ref):
    del te_ref
    valid = pl.program_id(1) < nv_ref[0]

    @pl.when(valid)
    def _():
        a = a_ref[...]
        g = _dot(a, wg_ref[...].astype(BF16))
        u = _dot(a, wu_ref[...].astype(BF16))
        o_ref[...] = (_silu(g) * u).astype(o_ref.dtype)

    @pl.when(jnp.logical_not(valid))
    def _():
        o_ref[...] = jnp.zeros_like(o_ref)


def _moe_gateup(a, wg, wu, layer, tile_expert, n_valid, tile):
    p, d = a.shape
    f = wg.shape[3]
    tn = _pick(f, (512, 256, 128))
    nt = p // tile

    def a_map(j, i, te, nv):
        return (jnp.minimum(i, nv[0] - 1), 0)

    def w_map(j, i, te, nv):
        return (layer, te[jnp.minimum(i, nv[0] - 1)], 0, j)

    def o_map(j, i, te, nv):
        return (i, j)

    grid_spec = pltpu.PrefetchScalarGridSpec(
        num_scalar_prefetch=2,
        grid=(f // tn, nt),
        in_specs=[
            pl.BlockSpec((tile, d), a_map),
            pl.BlockSpec((None, None, d, tn), w_map),
            pl.BlockSpec((None, None, d, tn), w_map),
        ],
        out_specs=pl.BlockSpec((tile, tn), o_map),
    )
    return _pallas(
        _moe_gateup_kernel,
        grid_spec=grid_spec,
        out_shape=jax.ShapeDtypeStruct((p, f), BF16),
        sem=("arbitrary", "arbitrary"),
    )(tile_expert, n_valid, a, wg, wu)


def _moe_down_kernel(te_ref, nv_ref, a_ref, w_ref, o_ref, acc_ref, *, nk):
    del te_ref
    k = pl.program_id(2)
    valid = pl.program_id(1) < nv_ref[0]

    @pl.when(valid & (k == 0))
    def _():
        acc_ref[...] = _dot(a_ref[...], w_ref[...])

    @pl.when(valid & (k > 0))
    def _():
        acc_ref[...] += _dot(a_ref[...], w_ref[...])

    @pl.when(valid & (k == nk - 1))
    def _():
        o_ref[...] = acc_ref[...]

    @pl.when(jnp.logical_not(valid) & (k == nk - 1))
    def _():
        o_ref[...] = jnp.zeros_like(o_ref)


def _moe_down(a, w, tile_expert, n_valid, tile):
    p, f = a.shape
    d = w.shape[2]
    tn = _pick(d, (2048, 1024, 512, 256, 128))
    tk = _pick(f, (2048, 1024, 512, 256, 128))
    nk = f // tk
    nt = p // tile

    def clamp(i, k, nv):
        ok = i < nv[0]
        return jnp.where(ok, i, nv[0] - 1), jnp.where(ok, k, nk - 1)

    def a_map(j, i, k, te, nv):
        ie, ke = clamp(i, k, nv)
        return (ie, ke)

    def w_map(j, i, k, te, nv):
        ie, ke = clamp(i, k, nv)
        return (te[ie], ke, j)

    def o_map(j, i, k, te, nv):
        return (i, j)

    grid_spec = pltpu.PrefetchScalarGridSpec(
        num_scalar_prefetch=2,
        grid=(d // tn, nt, nk),
        in_specs=[
            pl.BlockSpec((tile, tk), a_map),
            pl.BlockSpec((None, tk, tn), w_map),
        ],
        out_specs=pl.BlockSpec((tile, tn), o_map),
        scratch_shapes=[pltpu.VMEM((tile, tn), F32)],
    )
    return _pallas(
        functools.partial(_moe_down_kernel, nk=nk),
        grid_spec=grid_spec,
        out_shape=jax.ShapeDtypeStruct((p, d), F32),
        sem=("arbitrary", "arbitrary", "arbitrary"),
    )(tile_expert, n_valid, a, w)


def _combine_kernel(dest_ref, y_ref, tg_ref, x_ref, gt_ref, o_ref, buf_ref, sem):
    tb = x_ref.shape[0]

    def copy(t, k):
        return pltpu.make_async_copy(y_ref.at[pl.ds(dest_ref[0, 0, TOP_K * t + k], 1), :],
                                     buf_ref.at[k, pl.ds(t, 1), :], sem)

    def start(t, c):
        for k in range(TOP_K):
            copy(t, k).start()
        return c

    def wait(t, c):
        for k in range(TOP_K):
            copy(t, k).wait()
        return c

    lax.fori_loop(0, tb, start, 0, unroll=8)
    lax.fori_loop(0, tb, wait, 0, unroll=8)
    tg = tg_ref[...]
    y = tg[:, 0:1] * buf_ref[0] + tg[:, 1:2] * buf_ref[1]
    o_ref[...] = x_ref[...] + _rows(gt_ref[...], tb) * y


def _combine(ys, dest, tg, x, gate, group):
    m, d = x.shape
    tb = _pick(m, (256,))
    nt = m // tb
    gpt = tb // group
    row = lambda i: (i, 0)
    return _pallas(
        _combine_kernel,
        grid=(nt,),
        in_specs=[
            pl.BlockSpec((1, 1, TOP_K * tb), lambda i: (i, 0, 0), memory_space=pltpu.SMEM),
            pl.BlockSpec(memory_space=pl.ANY),
            pl.BlockSpec((tb, LANES), row),
            pl.BlockSpec((tb, d), row),
            pl.BlockSpec((gpt, d), row),
        ],
        out_specs=pl.BlockSpec((tb, d), row),
        out_shape=jax.ShapeDtypeStruct((m, d), F32),
        scratch_shapes=[pltpu.VMEM((TOP_K, tb, d), F32), pltpu.SemaphoreType.DMA(())],
        sem=("arbitrary",),
    )(dest.reshape(nt, 1, TOP_K * tb), ys, tg, x, gate)


def _routing(ti, n_experts, tile):
    m = ti.shape[0]
    n_slots = m * TOP_K
    e_flat = ti[:, :TOP_K].reshape(n_slots)
    onehot = (e_flat[:, None] == jnp.arange(n_experts, dtype=jnp.int32)[None, :]).astype(jnp.int32)
    csum = jnp.cumsum(onehot, axis=0)
    rank = jnp.sum((csum - onehot) * onehot, axis=1)
    counts = csum[-1]
    tiles_e = (counts + tile - 1) // tile
    padded = tiles_e * tile
    start_pad = jnp.cumsum(padded) - padded
    start_raw = jnp.cumsum(counts) - counts
    dest = start_pad[e_flat] + rank
    nt = n_slots // tile + n_experts
    p = nt * tile
    tile_end = jnp.cumsum(tiles_e)
    n_valid = tile_end[-1:].astype(jnp.int32)
    tile_expert = jnp.minimum(
        jnp.sum((jnp.arange(nt, dtype=jnp.int32)[:, None] >= tile_end[None, :]).astype(jnp.int32), axis=1),
        n_experts - 1).astype(jnp.int32)
    slot_token = jnp.arange(n_slots, dtype=jnp.int32) // TOP_K
    _, sorted_token = lax.sort_key_val(e_flat, slot_token, is_stable=True)
    src = jnp.concatenate([sorted_token, jnp.zeros((n_slots,), jnp.int32)])
    row_token = jnp.zeros((p + n_slots,), jnp.int32)
    for e in range(n_experts):
        seg = lax.dynamic_slice(src, (start_raw[e],), (n_slots,))
        row_token = lax.dynamic_update_slice(row_token, seg, (start_pad[e],))
    return dest.astype(jnp.int32), row_token[:p], tile_expert, n_valid


def kernel(x_prompt, x_sample, cache_a_k, cache_a_v, cache_b_k, cache_b_v, cache_c_k, cache_c_v, cache_c_logf, c_prompt, c_sample, ada_w, ada_b, norm_g, w_in_ab, rel_bias_a, sinks_b, w_out_ab, ffn_w_gate, ffn_w_up, ffn_w_down, w_in_c, b_forget_c, w_out_c, router_w, router_b, moe_w_gate, moe_w_up, moe_w_down, final_norm_g):
    batch, seq, d = x_prompt.shape
    db, t_new, _ = x_sample.shape
    depth = ada_w.shape[0]
    ha = cache_a_k.shape[3]
    hkv_b = cache_b_k.shape[3]
    hb = sinks_b.shape[1]
    hc = cache_c_k.shape[3]
    past = cache_c_k.shape[2]
    la = cache_a_k.shape[2]
    lb = cache_b_k.shape[2]
    n_experts = router_w.shape[2]
    hd = HEAD_DIM
    tp = batch * seq
    ts = db * t_new
    m = tp + ts
    group = t_new
    assert seq % Q_ROWS == 0 and Q_ROWS % CHUNK == 0 and A_PREV_CHUNKS * CHUNK <= Q_ROWS
    assert group % 8 == 0 and seq % group == 0 and t_new <= CHUNK and past % CHUNK == 0
    assert hc <= LANES and n_experts <= LANES and tp % Q_ROWS == 0

    x = jnp.concatenate([x_prompt.reshape(tp, d), x_sample.reshape(ts, d)], axis=0)

    n_req = batch + db
    req_pad = -(-n_req // 8) * 8
    c_all = jnp.zeros((req_pad, d), F32).at[:n_req].set(jnp.concatenate([c_prompt, c_sample], axis=0))
    mod = _adaln(c_all, ada_w, ada_b)
    group_req = np.concatenate([np.repeat(np.arange(batch), seq // group), batch + np.arange(db)])

    def mod_params(layer):
        per_group = mod[layer][group_req]
        return [per_group[:, i * d:(i + 1) * d] for i in range(6)]

    states_even, states_odd = [], []
    for layer in range(depth):
        i = layer // 2
        sh_m, sc_m, gt_m, sh_f, sc_f, gt_f = mod_params(layer)
        h = _modulate(x, norm_g[layer, 0], sc_m, sh_m, group, BF16)
        if layer % 2 == 0:
            z = _mm(h, w_in_ab, i, 0, m, 0, w_in_ab.shape[2])
            ca, cb = ha, 3 * ha
            n_heads = ha + hb
            kvg = hb // hkv_b
            qcol = lambda hh: jnp.where(hh < ha, hh, cb + hh - ha)
            kcol = lambda hh: jnp.where(hh < ha, ca + hh, cb + hb + (hh - ha) // kvg)
            vcol = lambda hh: jnp.where(hh < ha, 2 * ca + hh, cb + hb + hkv_b + (hh - ha) // kvg)
            q_pos = Q_ROWS + np.arange(Q_ROWS)
            k_pos = np.arange(2 * Q_ROWS)
            ba, bb = _bias_tables(rel_bias_a[i], hb, q_pos, k_pos)
            sinks = jnp.concatenate([jnp.full((ha,), NEG, F32), sinks_b[i].astype(F32)])
            o = _band_prompt(z, jnp.concatenate([ba, bb], axis=0), sinks, batch, seq, m, (qcol, kcol, vcol))
            q_pos = past + np.arange(t_new)
            sa, _ = _bias_tables(rel_bias_a[i], hb, q_pos, past - la + np.arange(la + t_new))
            _, sb = _bias_tables(rel_bias_a[i], hb, q_pos, past - lb + np.arange(lb + t_new))
            o = _band_sample(z, cache_a_k[i].reshape(db, la, ha * hd), cache_a_v[i].reshape(db, la, ha * hd),
                             sa[:, :, :la], sa[:, :, la:], jnp.full((ha,), NEG, F32), o, tp, t_new,
                             (lambda hh: hh, lambda hh: ca + hh, lambda hh: 2 * ca + hh), 1, 0)
            o = _band_sample(z, cache_b_k[i].reshape(db, lb, hkv_b * hd), cache_b_v[i].reshape(db, lb, hkv_b * hd),
                             sb[:, :, :lb], sb[:, :, lb:], sinks_b[i].astype(F32), o, tp, t_new,
                             (lambda hh: cb + hh, lambda hh: cb + hb + hh // kvg, lambda hh: cb + hb + hkv_b + hh // kvg),
                             kvg, ha)
            x = _mm_resid(o, w_out_ab, i, x, gt_m, group)

            zp = z[:tp].reshape(batch, seq, -1)
            zs = z[tp:].reshape(db, t_new, -1)
            wa, wb = ha * hd, hb * hd
            wkv = hkv_b * hd
            ka_p, va_p = zp[:, :, wa:2 * wa], zp[:, :, 2 * wa:3 * wa]
            kb_p, vb_p = zp[:, :, 3 * wa + wb:3 * wa + wb + wkv], zp[:, :, 3 * wa + wb + wkv:]
            ka_s, va_s = zs[:, :, wa:2 * wa], zs[:, :, 2 * wa:3 * wa]
            kb_s, vb_s = zs[:, :, 3 * wa + wb:3 * wa + wb + wkv], zs[:, :, 3 * wa + wb + wkv:]
            lap, lbp = min(la, seq), min(lb, seq)
            heads = lambda t, nh: t.reshape(t.shape[0], t.shape[1], nh, hd)
            roll = lambda cache, new, nh: jnp.concatenate([cache, heads(new, nh)], axis=1)[:, -cache.shape[1]:]
            states_even.append((
                heads(ka_p[:, -lap:], ha), heads(va_p[:, -lap:], ha),
                roll(cache_a_k[i], ka_s, ha), roll(cache_a_v[i], va_s, ha),
                heads(kb_p[:, -lbp:], hkv_b), heads(vb_p[:, -lbp:], hkv_b),
                roll(cache_b_k[i], kb_s, hkv_b), roll(cache_b_v[i], vb_s, hkv_b)))

            h = _modulate(x, norm_g[layer, 1], sc_f, sh_f, group, BF16)
            f = ffn_w_gate.shape[2]
            fpad = -(-f // 1024) * 1024 if f > 1024 else f
            padc = lambda w: jnp.pad(w.astype(BF16), ((0, 0), (0, fpad - f)))
            mid = _mm_swiglu(h, padc(ffn_w_gate[i]), padc(ffn_w_up[i]))
            wd = jnp.pad(ffn_w_down[i].astype(BF16), ((0, fpad - f), (0, 0)))
            x = _mm_resid(mid, wd[None], 0, x, gt_f, group)
        else:
            wc = hc * hd
            q_p, k_p, v_p = (_mm(h, w_in_c, i, 0, tp, c * wc, wc) for c in range(3))
            q_s, k_s, v_s = (_mm(h, w_in_c, i, tp, ts, c * wc, wc) for c in range(3))
            wgate = jnp.zeros((d, LANES), BF16).at[:, :hc].set(w_in_c[i][:, 3 * wc:].astype(BF16))
            bgate = jnp.zeros((1, LANES), F32).at[0, :hc].set(b_forget_c[i].astype(F32))
            logf = _mm_logsig(h, wgate, bgate)
            logf_p = logf[:tp].reshape(batch, seq, LANES)
            logf_s = logf[tp:].reshape(db, t_new, LANES)
            cum_p, = _cumsum([logf_p])
            cache_lf = jnp.pad(cache_c_logf[i].astype(F32), ((0, 0), (0, 0), (0, LANES - hc)))
            cum_c, cum_s = _cumsum([cache_lf, logf_s])
            nb = seq // Q_ROWS
            cum_t_p = jnp.swapaxes(cum_p[:, :, :hc], 1, 2).reshape(batch, hc, nb, Q_ROWS)
            o = _fox_prompt(q_p, k_p, v_p, cum_p.reshape(tp, LANES), cum_t_p, batch, seq, hc, m)
            cum_t_c = jnp.swapaxes(cum_c[:, :, :hc], 1, 2).reshape(db, hc, 1, past)
            cum_t_s = jnp.swapaxes(cum_s[:, :, :hc], 1, 2).reshape(db, hc, 1, t_new)
            o = _fox_sample(q_s, k_s, v_s, cache_c_k[i], cache_c_v[i],
                            cum_s.reshape(ts, LANES), cum_t_c, cum_t_s, o, tp, t_new, hc)
            x = _mm_resid(o, w_out_c, i, x, gt_m, group)

            states_odd.append((k_p.reshape(batch, seq, hc, hd), v_p.reshape(batch, seq, hc, hd), logf_p[:, :, :hc],
                               k_s.reshape(db, t_new, hc, hd), v_s.reshape(db, t_new, hc, hd), logf_s[:, :, :hc]))

            h32, ti, tg = _route(x, norm_g[layer, 1], sc_f, sh_f, router_w[i], router_b[i], group)
            tile = _pick(m * TOP_K, (512, 256))
            dest, row_token, tile_expert, n_valid = _routing(ti, n_experts, tile)
            hs = _gather_rows(h32, row_token, tile)
            mid = _moe_gateup(hs, moe_w_gate, moe_w_up, i, tile_expert, n_valid, tile)
            ys = _moe_down(mid, moe_w_down[i].astype(BF16), tile_expert, n_valid, tile)
            x = _combine(ys, dest, tg, x, gt_f, group)

    y_prompt = _final_norm(x, final_norm_g, 0, tp).reshape(batch, seq, d)
    y_sample = _final_norm(x, final_norm_g, tp, ts).reshape(db, t_new, d)
    even = [jnp.stack(s) for s in zip(*states_even)]
    odd = [jnp.stack(s) for s in zip(*states_odd)]
    return (y_prompt, y_sample, *even, *odd)
```
